```python
import jax, jax.numpy as jnp
from jax import lax
import numpy as np

D_MODEL = 4096
BATCH = 2
SEQ = 8192
DEPTH = 2
DEC_BATCH = 8
DEC_SEQ = 32
PAST_LEN = 2048

CHUNK = 64
N_MIXERS = 2
N_A_LAYERS = (DEPTH + 1) // 2
N_B_LAYERS = DEPTH // 2
HEAD_DIM = 128
N_HEADS = D_MODEL // HEAD_DIM
N_KV = 8
GQA_GROUP = N_HEADS // N_KV
Q_COLS = N_HEADS * HEAD_DIM
KV_COLS = N_KV * HEAD_DIM
ROT_DIM = HEAD_DIM // 4
ROPE_THETA = 500000.0
ATTN_SCALE = HEAD_DIM ** -0.5
WINDOW = 128
WINDOW_CHUNKS = WINDOW // CHUNK
IDX_HEADS = 16
IDX_DIM = 64
IDX_ROT = IDX_DIM // 4
IDX_SCALE = IDX_DIM ** -0.5
IDX_COLS = IDX_HEADS * IDX_DIM + IDX_DIM + IDX_HEADS
TOPK_MAX = 256
QBLOCK = 128
N_GROUPS = 8
EXP_PER_GROUP = 8
N_EXPERTS = N_GROUPS * EXP_PER_GROUP
TOP_IN_GROUP = 2
D_EXPERT = D_MODEL // 8
MOE_BLOCK = 256
RMS_EPS = 1e-6
F32 = jnp.float32

kernel_name = 'hybrid_swa_sink_dsa_hiermoe_stream_step'


def rms_norm(x, g):
    xf = x.astype(F32)
    y = xf * lax.rsqrt(jnp.mean(xf * xf, axis=-1, keepdims=True) + RMS_EPS)
    return (y * g.astype(F32)).astype(x.dtype)


def rotary_partial(x, pos, rot_dim):
    half = rot_dim // 2
    inv_freq = 1.0 / (ROPE_THETA ** (jnp.arange(half, dtype=F32) * (2.0 / rot_dim)))
    ang = pos.astype(F32)[:, None] * inv_freq[None, :]
    cos = jnp.cos(ang)[:, None, :]
    sin = jnp.sin(ang)[:, None, :]
    xf = x.astype(F32)
    x1 = xf[..., :half]
    x2 = xf[..., half:rot_dim]
    out = jnp.concatenate([x1 * cos - x2 * sin, x2 * cos + x1 * sin, xf[..., rot_dim:]], axis=-1)
    return out.astype(x.dtype)


def project_qkv(h, w_in, q_gain, k_gain, pos):
    b, s, _ = h.shape
    proj = h @ w_in
    q = proj[..., :Q_COLS].reshape(b, s, N_HEADS, HEAD_DIM)
    k = proj[..., Q_COLS:Q_COLS + KV_COLS].reshape(b, s, N_KV, HEAD_DIM)
    v = proj[..., Q_COLS + KV_COLS:Q_COLS + 2 * KV_COLS].reshape(b, s, N_KV, HEAD_DIM)
    q = rotary_partial(rms_norm(q, q_gain), pos, ROT_DIM)
    k = rotary_partial(rms_norm(k, k_gain), pos, ROT_DIM)
    return q, k, v, proj[..., Q_COLS + 2 * KV_COLS:]


def sink_softmax_values(s, v, sinks, eq):
    sk = sinks.astype(F32).reshape(N_KV, GQA_GROUP, 1)
    m = jnp.maximum(jnp.max(s, axis=-1), sk)
    p = jnp.exp(s - m[..., None])
    denom = jnp.sum(p, axis=-1) + jnp.exp(sk - m)
    return jnp.einsum(eq, (p / denom[..., None]).astype(v.dtype), v)


def swa_prompt(q, k, v, sinks):
    b, s, _, _ = q.shape
    nc = s // CHUNK
    nband = WINDOW_CHUNKS + 1
    qc = q.reshape(b, nc, CHUNK, N_KV, GQA_GROUP, HEAD_DIM)

    def band(t):
        tc = jnp.pad(t.reshape(b, nc, CHUNK, N_KV, HEAD_DIM),
                     ((0, 0), (WINDOW_CHUNKS, 0), (0, 0), (0, 0), (0, 0)))
        return jnp.concatenate([tc[:, j:j + nc] for j in range(nband)], axis=2)

    kb, vb = band(k), band(v)
    key_chunk = jnp.arange(nc)[:, None] + (jnp.arange(nband * CHUNK) // CHUNK)[None, :] - WINDOW_CHUNKS
    sc = jnp.einsum('bcqkgd,bcnkd->bckgqn', qc, kb, preferred_element_type=F32) * ATTN_SCALE
    sc = jnp.where((key_chunk >= 0)[None, :, None, None, None, :], sc, -jnp.inf)
    o = sink_softmax_values(sc, vb, sinks, 'bckgqn,bcnkd->bcqkgd')
    return o.reshape(b, s, Q_COLS)


def swa_sample(q, k, v, cache_k, cache_v, sinks):
    b, n, _, _ = q.shape
    k_all = jnp.concatenate([cache_k.astype(k.dtype), k], axis=1)
    v_all = jnp.concatenate([cache_v.astype(v.dtype), v], axis=1)
    qg = q.reshape(b, n, N_KV, GQA_GROUP, HEAD_DIM)
    sc = jnp.einsum('bqkgd,bnkd->bkgqn', qg, k_all, preferred_element_type=F32) * ATTN_SCALE
    o = sink_softmax_values(sc, v_all, sinks, 'bkgqn,bnkd->bqkgd')
    return o.reshape(b, n, Q_COLS), k_all[:, -WINDOW:], v_all[:, -WINDOW:]


def indexer_features(rest, kidx_gain, pos):
    b, s, _ = rest.shape
    nq = IDX_HEADS * IDX_DIM
    q_idx = rest[..., :nq].reshape(b, s, IDX_HEADS, IDX_DIM)
    k_idx = rest[..., nq:nq + IDX_DIM]
    w_idx = rest[..., nq + IDX_DIM:].astype(F32) * (IDX_HEADS ** -0.5)
    q_idx = rotary_partial(q_idx, pos, IDX_ROT)
    k_idx = rotary_partial(rms_norm(k_idx, kidx_gain)[:, :, None, :], pos, IDX_ROT)[:, :, 0, :]
    return q_idx, k_idx, w_idx


def index_scores(q_idx, k_idx, w_idx):
    dots = jnp.einsum('bqhd,bld->bqhl', q_idx, k_idx, preferred_element_type=F32) * IDX_SCALE
    return jnp.einsum('bqhl,bqh->bql', jax.nn.relu(dots), w_idx)


def sparse_attend(q, k, v, idx, valid):
    b, nq = q.shape[:2]
    ks = jax.vmap(lambda kk, ii: kk[ii])(k, idx)
    vs = jax.vmap(lambda vv, ii: vv[ii])(v, idx)
    qg = q.reshape(b, nq, N_KV, GQA_GROUP, HEAD_DIM)
    sc = jnp.einsum('bqkgd,bqnkd->bqkgn', qg, ks, preferred_element_type=F32) * ATTN_SCALE
    sc = jnp.where(valid[:, :, None, None, :], sc, -jnp.inf)
    p = jax.nn.softmax(sc, axis=-1).astype(v.dtype)
    o = jnp.einsum('bqkgn,bqnkd->bqkgd', p, vs)
    return o.reshape(b, nq, Q_COLS)


def dsa_prompt(q, k, v, q_idx, k_idx, w_idx):
    b, s = q.shape[:2]
    nb = s // QBLOCK
    topk = min(TOPK_MAX, s // 4)
    key_chunk = jnp.arange(s) // CHUNK

    def to_blocks(t):
        return jnp.swapaxes(t.reshape((b, nb, QBLOCK) + t.shape[2:]), 0, 1)

    def one_block(args):
        qb, qib, wb, blk = args
        q_chunk = (blk * QBLOCK + jnp.arange(QBLOCK)) // CHUNK
        adm = key_chunk[None, :] <= q_chunk[:, None]
        sc = jnp.where(adm[None], index_scores(qib, k_idx, wb), -jnp.inf)
        val, idx = lax.top_k(sc, topk)
        return sparse_attend(qb, k, v, idx, jnp.isfinite(val))

    out = lax.map(one_block, (to_blocks(q), to_blocks(q_idx), to_blocks(w_idx), jnp.arange(nb)))
    return jnp.swapaxes(out, 0, 1).reshape(b, s, Q_COLS)


def dsa_sample(q, k, v, q_idx, k_idx, w_idx, cache_k, cache_v, cache_kidx):
    k_all = jnp.concatenate([cache_k.astype(k.dtype), k], axis=1)
    v_all = jnp.concatenate([cache_v.astype(v.dtype), v], axis=1)
    kidx_all = jnp.concatenate([cache_kidx.astype(k_idx.dtype), k_idx], axis=1)
    topk = min(TOPK_MAX, k_all.shape[1] // 4)
    val, idx = lax.top_k(index_scores(q_idx, kidx_all, w_idx), topk)
    return sparse_attend(q, k_all, v_all, idx, jnp.isfinite(val))


def grouped_experts(xt, e, g, w_gate, w_up, w_down):
    T, D = xt.shape
    A = e.shape[0]
    tok = jnp.arange(A) // TOP_IN_GROUP
    order = jnp.argsort(e)
    e_s, tok_s, g_s = e[order], tok[order], g[order]
    counts = jnp.zeros((N_EXPERTS,), jnp.int32).at[e].add(1)
    start = jnp.cumsum(counts) - counts
    pcounts = (counts + MOE_BLOCK - 1) // MOE_BLOCK * MOE_BLOCK
    pend = jnp.cumsum(pcounts)
    pstart = pend - pcounts
    dest = pstart[e_s] + (jnp.arange(A) - start[e_s])
    nblk = (A + MOE_BLOCK - 1) // MOE_BLOCK + N_EXPERTS
    P = nblk * MOE_BLOCK
    tok_buf = jnp.full((P,), T, jnp.int32).at[dest].set(tok_s)
    gate_buf = jnp.zeros((P,), F32).at[dest].set(g_s)
    blk_exp = jnp.minimum(jnp.searchsorted(pend, jnp.arange(nblk) * MOE_BLOCK, side='right'), N_EXPERTS - 1)
    x_pad = jnp.concatenate([xt, jnp.zeros((1, D), xt.dtype)], axis=0)

    def step(acc, args):
        ti, gi, ei = args
        xb = x_pad[ti]
        y = (jax.nn.silu(xb @ w_gate[ei]) * (xb @ w_up[ei])) @ w_down[ei]
        return acc.at[ti].add(y.astype(F32) * gi[:, None]), None

    acc, _ = lax.scan(step, jnp.zeros((T + 1, D), F32),
                      (tok_buf.reshape(nblk, MOE_BLOCK), gate_buf.reshape(nblk, MOE_BLOCK), blk_exp))
    return acc[:T].astype(xt.dtype)


def hier_moe(h, w_group, b_group, w_expert, b_expert, w_gate, w_up, w_down):
    b, s, D = h.shape
    T = b * s
    xt = h.reshape(T, D)
    glog = (xt @ w_group).astype(F32) + b_group.astype(F32)
    gprob = jax.nn.softmax(glog, axis=-1)
    gsel = jnp.argmax(glog, axis=-1)
    elog = ((xt @ w_expert).astype(F32) + b_expert.astype(F32)).reshape(T, N_GROUPS, EXP_PER_GROUP)
    elog_sel = jnp.take_along_axis(elog, gsel[:, None, None], axis=1)[:, 0]
    top_v, top_i = lax.top_k(elog_sel, TOP_IN_GROUP)
    gate = jnp.take_along_axis(gprob, gsel[:, None], axis=1) * jax.nn.softmax(top_v, axis=-1)
    expert = gsel[:, None] * EXP_PER_GROUP + top_i
    y = grouped_experts(xt, expert.reshape(-1).astype(jnp.int32), gate.reshape(-1), w_gate, w_up, w_down)
    return y.reshape(b, s, D)


def setup_inputs(seed: int = 0) -> dict:
    key = jax.random.key(seed)
    ks = jax.random.split(key, 26)

    def nrm(k, shape, scale):
        return jax.random.normal(k, shape, F32) * scale

    qkv_a = Q_COLS + 2 * KV_COLS
    qkv_b = qkv_a + IDX_COLS
    return {
        'x_prompt': nrm(ks[0], (BATCH, SEQ, D_MODEL), 1.0),
        'x_sample': nrm(ks[1], (DEC_BATCH, DEC_SEQ, D_MODEL), 1.0),
        'cache_a_k': nrm(ks[2], (N_A_LAYERS, DEC_BATCH, WINDOW, N_KV, HEAD_DIM), 1.0),
        'cache_a_v': nrm(ks[3], (N_A_LAYERS, DEC_BATCH, WINDOW, N_KV, HEAD_DIM), 1.0),
        'cache_b_k': nrm(ks[4], (N_B_LAYERS, DEC_BATCH, PAST_LEN, N_KV, HEAD_DIM), 1.0),
        'cache_b_v': nrm(ks[5], (N_B_LAYERS, DEC_BATCH, PAST_LEN, N_KV, HEAD_DIM), 1.0),
        'cache_b_kidx': nrm(ks[6], (N_B_LAYERS, DEC_BATCH, PAST_LEN, IDX_DIM), 1.0),
        'attn_norm': 1.0 + nrm(ks[7], (DEPTH, D_MODEL), 0.01),
        'ffn_norm': 1.0 + nrm(ks[8], (DEPTH, D_MODEL), 0.01),
        'a_w_qkv': nrm(ks[9], (N_A_LAYERS, D_MODEL, qkv_a), D_MODEL ** -0.5),
        'a_w_o': nrm(ks[10], (N_A_LAYERS, Q_COLS, D_MODEL), Q_COLS ** -0.5),
        'a_q_norm': 1.0 + nrm(ks[11], (N_A_LAYERS, HEAD_DIM), 0.01),
        'a_k_norm': 1.0 + nrm(ks[12], (N_A_LAYERS, HEAD_DIM), 0.01),
        'a_sinks': nrm(ks[13], (N_A_LAYERS, N_HEADS), 1.0),
        'b_w_qkv': nrm(ks[14], (N_B_LAYERS, D_MODEL, qkv_b), D_MODEL ** -0.5),
        'b_w_o': nrm(ks[15], (N_B_LAYERS, Q_COLS, D_MODEL), Q_COLS ** -0.5),
        'b_q_norm': 1.0 + nrm(ks[16], (N_B_LAYERS, HEAD_DIM), 0.01),
        'b_k_norm': 1.0 + nrm(ks[17], (N_B_LAYERS, HEAD_DIM), 0.01),
        'b_kidx_norm': 1.0 + nrm(ks[18], (N_B_LAYERS, IDX_DIM), 0.01),
        'moe_w_group': nrm(ks[19], (DEPTH, D_MODEL, N_GROUPS), D_MODEL ** -0.5),
        'moe_b_group': nrm(ks[20], (DEPTH, N_GROUPS), 0.01),
        'moe_w_expert': nrm(ks[21], (DEPTH, D_MODEL, N_EXPERTS), D_MODEL ** -0.5),
        'moe_b_expert': nrm(ks[22], (DEPTH, N_EXPERTS), 0.01),
        'moe_w_gate': nrm(ks[23], (DEPTH, N_EXPERTS, D_MODEL, D_EXPERT), D_MODEL ** -0.5),
        'moe_w_up': nrm(ks[24], (DEPTH, N_EXPERTS, D_MODEL, D_EXPERT), D_MODEL ** -0.5),
        'moe_w_down': nrm(ks[25], (DEPTH, N_EXPERTS, D_EXPERT, D_MODEL), D_EXPERT ** -0.5),
    }


def reference(x_prompt, x_sample, cache_a_k, cache_a_v, cache_b_k, cache_b_v, cache_b_kidx,
              attn_norm, ffn_norm, a_w_qkv, a_w_o, a_q_norm, a_k_norm, a_sinks,
              b_w_qkv, b_w_o, b_q_norm, b_k_norm, b_kidx_norm,
              moe_w_group, moe_b_group, moe_w_expert, moe_b_expert, moe_w_gate, moe_w_up, moe_w_down):
    pos_p = jnp.arange(x_prompt.shape[1])
    pos_s = PAST_LEN + jnp.arange(x_sample.shape[1])
    hp, hs = x_prompt, x_sample
    ak_p, av_p, ak_s, av_s = [], [], [], []
    bk_p, bv_p, bi_p, bk_s, bv_s, bi_s = [], [], [], [], [], []
    for i in range(DEPTH):
        j = i // N_MIXERS
        np_ = rms_norm(hp, attn_norm[i])
        ns_ = rms_norm(hs, attn_norm[i])
        if i % N_MIXERS == 0:
            q, k, v, _ = project_qkv(np_, a_w_qkv[j], a_q_norm[j], a_k_norm[j], pos_p)
            op = swa_prompt(q, k, v, a_sinks[j])
            ak_p.append(k[:, -WINDOW:])
            av_p.append(v[:, -WINDOW:])
            q, k, v, _ = project_qkv(ns_, a_w_qkv[j], a_q_norm[j], a_k_norm[j], pos_s)
            os_, k_buf, v_buf = swa_sample(q, k, v, cache_a_k[j], cache_a_v[j], a_sinks[j])
            ak_s.append(k_buf)
            av_s.append(v_buf)
            w_o = a_w_o[j]
        else:
            q, k, v, rest = project_qkv(np_, b_w_qkv[j], b_q_norm[j], b_k_norm[j], pos_p)
            qi, ki, wi = indexer_features(rest, b_kidx_norm[j], pos_p)
            op = dsa_prompt(q, k, v, qi, ki, wi)
            bk_p.append(k)
            bv_p.append(v)
            bi_p.append(ki)
            q, k, v, rest = project_qkv(ns_, b_w_qkv[j], b_q_norm[j], b_k_norm[j], pos_s)
            qi, ki, wi = indexer_features(rest, b_kidx_norm[j], pos_s)
            os_ = dsa_sample(q, k, v, qi, ki, wi, cache_b_k[j], cache_b_v[j], cache_b_kidx[j])
            bk_s.append(k)
            bv_s.append(v)
            bi_s.append(ki)
            w_o = b_w_o[j]
        hp = hp + op @ w_o
        hs = hs + os_ @ w_o
        hp = hp + hier_moe(rms_norm(hp, ffn_norm[i]), moe_w_group[i], moe_b_group[i], moe_w_expert[i],
                           moe_b_expert[i], moe_w_gate[i], moe_w_up[i], moe_w_down[i])
        hs = hs + hier_moe(rms_norm(hs, ffn_norm[i]), moe_w_group[i], moe_b_group[i], moe_w_expert[i],
                           moe_b_expert[i], moe_w_gate[i], moe_w_up[i], moe_w_down[i])
    return (hp, hs,
            jnp.stack(ak_p), jnp.stack(av_p), jnp.stack(bk_p), jnp.stack(bv_p), jnp.stack(bi_p),
            jnp.stack(ak_s), jnp.stack(av_s), jnp.stack(bk_s), jnp.stack(bv_s), jnp.stack(bi_s))
```

```python
import functools

import jax
import jax.numpy as jnp
from jax import lax
from jax.experimental import pallas as pl
from jax.experimental.pallas import tpu as pltpu

F32 = jnp.float32
BF16 = jnp.bfloat16
I32 = jnp.int32

LANES = 128
CHUNK = 64
HEAD_DIM = 128
N_KV = 8
WINDOW = 128
ROT_DIM = HEAD_DIM // 4
ROPE_THETA = 500000.0
IDX_HEADS = 16
IDX_DIM = 64
IDX_ROT = IDX_DIM // 4
TOPK_MAX = 256
QBLOCK = 128
N_GROUPS = 8
EXP_PER_GROUP = 8
N_EXPERTS = N_GROUPS * EXP_PER_GROUP
RMS_EPS = 1e-6
NEG = -1e30
INT_MIN = -2 ** 31
KEY_TILE = 256
MOE_ROWS = 256
VMEM_LIMIT = 56 * 1024 * 1024

_NT = (((1,), (1,)), ((), ()))


def _cparams(*sem):
    return pltpu.CompilerParams(dimension_semantics=sem, vmem_limit_bytes=VMEM_LIMIT)


def _pick_tile(n, target, mult):
    best = None
    t = mult
    while t <= min(n, target):
        if n % t == 0:
            best = t
        t += mult
    return best if best is not None else n


def _rmsnorm_kernel(x_ref, g_ref, *o_refs):
    x = x_ref[...]
    y = x * lax.rsqrt(jnp.mean(x * x, axis=-1, keepdims=True) + RMS_EPS) * g_ref[...]
    for o_ref in o_refs:
        o_ref[...] = y.astype(o_ref.dtype)


def rmsnorm(x, g, out_dtypes):
    t, d = x.shape
    tr = _pick_tile(t, 256, 16)
    outs = pl.pallas_call(
        _rmsnorm_kernel,
        grid=(t // tr,),
        in_specs=[pl.BlockSpec((tr, d), lambda i: (i, 0)),
                  pl.BlockSpec((1, d), lambda i: (0, 0))],
        out_specs=[pl.BlockSpec((tr, d), lambda i: (i, 0)) for _ in out_dtypes],
        out_shape=[jax.ShapeDtypeStruct((t, d), dt) for dt in out_dtypes],
        compiler_params=_cparams("parallel"),
        name="rmsnorm",
    )(x, g.reshape(1, d).astype(F32))
    return outs


def _mm_kernel(a_ref, b_ref, o_ref):
    o_ref[...] = jnp.dot(a_ref[...], b_ref[...], preferred_element_type=F32).astype(o_ref.dtype)


def _mm_res_kernel(a_ref, b_ref, r_ref, o_ref):
    o_ref[...] = r_ref[...] + jnp.dot(a_ref[...], b_ref[...], preferred_element_type=F32)


def matmul(a, b, res=None, tm_target=1280, tn_target=512, name="matmul"):
    m, k = a.shape
    _, n = b.shape
    tm = _pick_tile(m, tm_target, 16)
    tn = _pick_tile(n, tn_target, LANES)
    in_specs = [pl.BlockSpec((tm, k), lambda i, j: (i, 0)),
                pl.BlockSpec((k, tn), lambda i, j: (0, j))]
    args = [a, b]
    if res is not None:
        in_specs.append(pl.BlockSpec((tm, tn), lambda i, j: (i, j)))
        args.append(res)
    return pl.pallas_call(
        _mm_kernel if res is None else _mm_res_kernel,
        grid=(m // tm, n // tn),
        in_specs=in_specs,
        out_specs=pl.BlockSpec((tm, tn), lambda i, j: (i, j)),
        out_shape=jax.ShapeDtypeStruct((m, n), F32),
        compiler_params=_cparams("parallel", "arbitrary"),
        name=name,
    )(*args)


def rope_tables(pos, rot_dim, period):
    half = rot_dim // 2
    inv_freq = 1.0 / (ROPE_THETA ** (jnp.arange(half, dtype=F32) * (2.0 / rot_dim)))
    ang = pos.astype(F32)[:, None] * inv_freq[None, :]
    cos, sin = jnp.cos(ang), jnp.sin(ang)
    t = pos.shape[0]
    rest = period - rot_dim
    c = jnp.concatenate([cos, cos, jnp.ones((t, rest), F32)], axis=1)
    s1 = jnp.concatenate([-sin, jnp.zeros((t, half + rest), F32)], axis=1)
    s2 = jnp.concatenate([jnp.zeros((t, half), F32), sin, jnp.zeros((t, rest), F32)], axis=1)
    rep = LANES // period
    return tuple(jnp.tile(x, (1, rep)) for x in (c, s1, s2))


def _rope(y, c, s1, s2, half):
    return y * c + pltpu.roll(y, LANES - half, 1) * s1 + pltpu.roll(y, half, 1) * s2


def _qkv_post_kernel(p_ref, qg_ref, kg_ref, c_ref, s1_ref, s2_ref,
                     q_ref, kf_ref, kb_ref, vf_ref, vb_ref, *, n_heads, q_scale):
    c, s1, s2 = c_ref[...], s1_ref[...], s2_ref[...]

    def normrope(x, g):
        y = x * lax.rsqrt(jnp.mean(x * x, axis=-1, keepdims=True) + RMS_EPS) * g
        return _rope(y, c, s1, s2, ROT_DIM // 2)

    qg, kg = qg_ref[...], kg_ref[...]
    for h in range(n_heads):
        sl = slice(h * HEAD_DIM, (h + 1) * HEAD_DIM)
        q_ref[:, sl] = (normrope(p_ref[:, sl], qg) * q_scale).astype(BF16)
    for h in range(N_KV):
        sl = slice(h * HEAD_DIM, (h + 1) * HEAD_DIM)
        k = normrope(p_ref[:, (n_heads + h) * HEAD_DIM:(n_heads + h + 1) * HEAD_DIM], kg)
        kf_ref[:, sl] = k
        kb_ref[:, sl] = k.astype(BF16)
        v = p_ref[:, (n_heads + N_KV + h) * HEAD_DIM:(n_heads + N_KV + h + 1) * HEAD_DIM]
        vf_ref[:, sl] = v
        vb_ref[:, sl] = v.astype(BF16)


def qkv_post(proj, q_gain, k_gain, tables, n_heads):
    t, ncol = proj.shape
    tr = _pick_tile(t, 256, 16)
    kvc = N_KV * HEAD_DIM
    qc = n_heads * HEAD_DIM
    row = lambda w: pl.BlockSpec((tr, w), lambda i: (i, 0))
    vec = pl.BlockSpec((1, HEAD_DIM), lambda i: (0, 0))
    return pl.pallas_call(
        functools.partial(_qkv_post_kernel, n_heads=n_heads, q_scale=HEAD_DIM ** -0.5),
        grid=(t // tr,),
        in_specs=[row(ncol), vec, vec, row(LANES), row(LANES), row(LANES)],
        out_specs=[row(qc), row(kvc), row(kvc), row(kvc), row(kvc)],
        out_shape=[jax.ShapeDtypeStruct((t, qc), BF16),
                   jax.ShapeDtypeStruct((t, kvc), F32), jax.ShapeDtypeStruct((t, kvc), BF16),
                   jax.ShapeDtypeStruct((t, kvc), F32), jax.ShapeDtypeStruct((t, kvc), BF16)],
        compiler_params=_cparams("parallel"),
        name="qkv_post",
    )(proj, q_gain.reshape(1, HEAD_DIM).astype(F32), k_gain.reshape(1, HEAD_DIM).astype(F32), *tables)


IDX_W_SCALE = (IDX_HEADS ** -0.5) * (IDX_DIM ** -0.5)


def _idx_post_kernel(r_ref, g_ref, c_ref, s1_ref, s2_ref, qi_ref, tail_ref):
    c, s1, s2 = c_ref[...], s1_ref[...], s2_ref[...]
    nq = IDX_HEADS * IDX_DIM
    for j in range(nq // LANES):
        sl = slice(j * LANES, (j + 1) * LANES)
        qi_ref[:, sl] = _rope(r_ref[:, sl], c, s1, s2, IDX_ROT // 2).astype(BF16)
    t = r_ref[:, nq:nq + LANES]
    lane = lax.broadcasted_iota(I32, t.shape, 1)
    is_k = lane < IDX_DIM
    ms = jnp.sum(jnp.where(is_k, t * t, 0.0), axis=-1, keepdims=True) * (1.0 / IDX_DIM)
    kn = t * lax.rsqrt(ms + RMS_EPS) * g_ref[...]
    tail_ref[...] = jnp.where(is_k, _rope(kn, c, s1, s2, IDX_ROT // 2), t * IDX_W_SCALE)


def idx_post(rest, kidx_gain, tables):
    t, ncol = rest.shape
    tr = _pick_tile(t, 256, 16)
    nq = IDX_HEADS * IDX_DIM
    row = lambda w: pl.BlockSpec((tr, w), lambda i: (i, 0))
    g = jnp.concatenate([kidx_gain.astype(F32), jnp.zeros((LANES - IDX_DIM,), F32)]).reshape(1, LANES)
    return pl.pallas_call(
        _idx_post_kernel,
        grid=(t // tr,),
        in_specs=[row(ncol), pl.BlockSpec((1, LANES), lambda i: (0, 0)), row(LANES), row(LANES), row(LANES)],
        out_specs=[row(nq), row(LANES)],
        out_shape=[jax.ShapeDtypeStruct((t, nq), BF16), jax.ShapeDtypeStruct((t, LANES), F32)],
        compiler_params=_cparams("parallel"),
        name="idx_post",
    )(rest, g, *tables)


def _sink_attn(q, k, v, bias, sink):
    s = lax.dot_general(q, k, _NT, preferred_element_type=F32)
    if bias is not None:
        s = s + bias
    m = jnp.maximum(jnp.max(s, axis=-1, keepdims=True), sink)
    p = jnp.exp(s - m)
    denom = jnp.sum(p, axis=-1, keepdims=True) + jnp.exp(sink - m)
    o = jnp.dot(p.astype(BF16), v, preferred_element_type=F32)
    return o / denom


def _swa_prompt_kernel(sink_ref, q_ref, kp_ref, kc_ref, vp_ref, vc_ref, o_ref, *, group):
    i = pl.program_id(1)
    rows, cols = QBLOCK, 2 * QBLOCK
    qc = lax.broadcasted_iota(I32, (rows, cols), 0) // CHUNK
    kc = lax.broadcasted_iota(I32, (rows, cols), 1) // CHUNK
    first_valid = jnp.where(i > 0, 0, QBLOCK // CHUNK)
    vis = (kc >= qc) & (kc <= qc + WINDOW // CHUNK) & (kc >= first_valid)
    bias = jnp.where(vis, 0.0, NEG)
    for kv in range(N_KV):
        sl = slice(kv * HEAD_DIM, (kv + 1) * HEAD_DIM)
        k = jnp.concatenate([kp_ref[:, sl], kc_ref[:, sl]], axis=0)
        v = jnp.concatenate([vp_ref[:, sl], vc_ref[:, sl]], axis=0)
        for g in range(group):
            h = kv * group + g
            hs = slice(h * HEAD_DIM, (h + 1) * HEAD_DIM)
            o_ref[:, hs] = _sink_attn(q_ref[:, hs], k, v, bias, sink_ref[h]).astype(BF16)


def swa_prompt(q, k, v, sinks, n_batch, seq, n_heads):
    nqb = seq // QBLOCK
    qc = n_heads * HEAD_DIM
    kvc = N_KV * HEAD_DIM
    cur = lambda w: pl.BlockSpec((QBLOCK, w), lambda b, i: (b * nqb + i, 0))
    prev = lambda w: pl.BlockSpec((QBLOCK, w), lambda b, i: (b * nqb + jnp.maximum(i - 1, 0), 0))
    return pl.pallas_call(
        functools.partial(_swa_prompt_kernel, group=n_heads // N_KV),
        grid=(n_batch, nqb),
        in_specs=[pl.BlockSpec(memory_space=pltpu.SMEM),
                  cur(qc), prev(kvc), cur(kvc), prev(kvc), cur(kvc)],
        out_specs=cur(qc),
        out_shape=jax.ShapeDtypeStruct((n_batch * seq, qc), BF16),
        compiler_params=_cparams("parallel", "arbitrary"),
        name="swa_prompt",
    )(sinks.astype(F32), q, k, k, v, v)


def _swa_sample_kernel(sink_ref, q_ref, k_ref, v_ref, o_ref, *, group):
    for kv in range(N_KV):
        sl = slice(kv * HEAD_DIM, (kv + 1) * HEAD_DIM)
        k, v = k_ref[:, sl], v_ref[:, sl]
        for g in range(group):
            h = kv * group + g
            hs = slice(h * HEAD_DIM, (h + 1) * HEAD_DIM)
            o_ref[:, hs] = _sink_attn(q_ref[:, hs], k, v, None, sink_ref[h]).astype(BF16)


def swa_sample(q, k_all, v_all, sinks, row0, n_batch, n_new, n_heads):
    qc = n_heads * HEAD_DIM
    nk = k_all.shape[1]
    kvc = N_KV * HEAD_DIM
    return pl.pallas_call(
        functools.partial(_swa_sample_kernel, group=n_heads // N_KV),
        grid=(n_batch,),
        in_specs=[pl.BlockSpec(memory_space=pltpu.SMEM),
                  pl.BlockSpec((n_new, qc), lambda b: (row0 // n_new + b, 0)),
                  pl.BlockSpec((None, nk, kvc), lambda b: (b, 0, 0)),
                  pl.BlockSpec((None, nk, kvc), lambda b: (b, 0, 0))],
        out_specs=pl.BlockSpec((n_new, qc), lambda b: (b, 0)),
        out_shape=jax.ShapeDtypeStruct((n_batch * n_new, qc), BF16),
        compiler_params=_cparams("parallel"),
        name="swa_sample",
    )(sinks.astype(F32), q, k_all, v_all)


def _select_kernel(nkt_ref, qi_ref, w_ref, lim_ref, kidx_ref, bias_ref, keys_ref, p_ref,
                   *, topk, n_tiles, idx_bits):
    tk = KEY_TILE
    blk = pl.program_id(0) * pl.num_programs(1) + pl.program_id(1)
    nkt = nkt_ref[blk]
    qi = qi_ref[...]
    w = w_ref[...]
    lim = lim_ref[...]
    imin = jnp.int32(INT_MIN)

    def tile_rows(kt):
        return pl.ds(pl.multiple_of(kt * tk, tk), tk)

    def key_index(kt):
        return kt * tk + lax.broadcasted_iota(I32, (tk, LANES), 0)

    def score_tile(kt, carry):
        d = lax.dot_general(kidx_ref[tile_rows(kt), :], qi, _NT, preferred_element_type=F32)
        r = jnp.maximum(d, 0.0) * w
        acc = r[:, 0:LANES]
        for h in range(1, IDX_HEADS):
            acc = acc + r[:, h * LANES:(h + 1) * LANES]
        bits = pltpu.bitcast(acc, I32)
        key = bits ^ ((bits >> 31) & jnp.int32(0x7FFFFFFF))
        keys_ref[tile_rows(kt), :] = jnp.where(key_index(kt) < lim, key, imin)
        return carry

    lax.fori_loop(0, nkt, score_tile, 0)

    def count(pred):
        def body(kt, acc):
            c = jnp.where(pred(keys_ref[tile_rows(kt), :], key_index(kt)), 1, 0).astype(I32)
            return acc + jnp.sum(c.reshape(tk // 8, 8, LANES), axis=0)
        acc = lax.fori_loop(0, nkt, body, jnp.zeros((8, LANES), I32))
        return jnp.sum(acc, axis=0, keepdims=True)

    def value_bit(i, v):
        cand = v + lax.shift_left(jnp.int32(1), 31 - i)
        cnt = count(lambda kk, _: kk >= cand)
        return jnp.where(cnt >= topk, cand, v)

    tau = lax.fori_loop(0, 32, value_bit, jnp.full((1, LANES), imin, I32))

    n_ge = count(lambda kk, _: (kk >= tau) & (kk != imin))
    n_gt = count(lambda kk, _: kk > tau)
    need = topk - n_gt

    p_ref[...] = jnp.full((1, LANES), 2 ** 30, I32)

    @pl.when(jnp.max(n_ge) > topk)
    def _():
        def index_bit(i, p):
            cand = p + lax.shift_left(jnp.int32(1), idx_bits - 1 - i)
            cnt = count(lambda kk, si: (kk == tau) & (si < cand))
            return jnp.where(cnt < need, cand, p)
        p_ref[...] = lax.fori_loop(0, idx_bits, index_bit, jnp.zeros((1, LANES), I32))

    p_last = p_ref[...]

    def write_tile(kt, carry):
        kk = keys_ref[tile_rows(kt), :]
        sel = (kk > tau) | ((kk == tau) & (key_index(kt) <= p_last) & (kk != imin))
        bias_ref[tile_rows(kt), :] = jnp.where(sel, 0.0, NEG).astype(BF16)
        return carry

    lax.fori_loop(0, nkt, write_tile, 0)

    def fill_tile(kt, carry):
        bias_ref[tile_rows(kt), :] = jnp.full((tk, LANES), NEG, BF16)
        return carry

    lax.fori_loop(nkt, n_tiles, fill_tile, 0)


def dsa_select(nkt, qi_blocks, w_blocks, limits, kidx, topk):
    nb, lk, _ = kidx.shape
    nq = qi_blocks.shape[0] // nb
    n_tiles = lk // KEY_TILE
    idx_bits = max(1, int(lk).bit_length())
    per_block = lambda shape: pl.BlockSpec((None,) + shape, lambda b, q, *_: (b * nq + q, 0, 0))
    grid_spec = pltpu.PrefetchScalarGridSpec(
        num_scalar_prefetch=1,
        grid=(nb, nq),
        in_specs=[per_block((IDX_HEADS * QBLOCK, IDX_DIM)),
                  per_block((1, IDX_HEADS * QBLOCK)),
                  per_block((1, LANES)),
                  pl.BlockSpec((None, lk, IDX_DIM), lambda b, q, *_: (b, 0, 0))],
        out_specs=per_block((lk, LANES)),
        scratch_shapes=[pltpu.VMEM((lk, LANES), I32), pltpu.VMEM((1, LANES), I32)],
    )
    return pl.pallas_call(
        functools.partial(_select_kernel, topk=topk, n_tiles=n_tiles, idx_bits=idx_bits),
        grid_spec=grid_spec,
        out_shape=jax.ShapeDtypeStruct((nb * nq, lk, LANES), BF16),
        compiler_params=_cparams("parallel", "arbitrary"),
        name="dsa_select",
    )(nkt, qi_blocks, w_blocks, limits, kidx)


def _masked_attn_kernel(nkt_ref, q_ref, k_ref, v_ref, bias_ref, o_ref, m_ref, acc_ref, *, group):
    tk = KEY_TILE
    blk = pl.program_id(0) * pl.num_programs(2) + pl.program_id(2)
    nkt = nkt_ref[blk]
    rows = group * QBLOCK
    eye = jnp.where(lax.broadcasted_iota(I32, (QBLOCK, QBLOCK), 0)
                    == lax.broadcasted_iota(I32, (QBLOCK, QBLOCK), 1), 1.0, 0.0).astype(BF16)
    q_aug = jnp.concatenate(
        [jnp.concatenate([q_ref[:, g * HEAD_DIM:(g + 1) * HEAD_DIM], eye], axis=1) for g in range(group)],
        axis=0)
    ones = jnp.ones((tk, HEAD_DIM), BF16)
    m_ref[...] = jnp.full((rows, LANES), NEG, F32)
    acc_ref[...] = jnp.zeros((rows, 2 * HEAD_DIM), F32)

    def body(kt, carry):
        sl = pl.ds(pl.multiple_of(kt * tk, tk), tk)
        k_aug = jnp.concatenate([k_ref[sl, :], bias_ref[sl, :]], axis=1)
        s = lax.dot_general(q_aug, k_aug, _NT, preferred_element_type=F32)
        m_prev = m_ref[...]
        m_new = jnp.maximum(m_prev, jnp.max(s, axis=-1, keepdims=True))
        alpha = jnp.exp(m_prev - m_new)
        p = jnp.exp(s - jnp.concatenate([m_new] * (tk // LANES), axis=1))
        v_aug = jnp.concatenate([v_ref[sl, :], ones], axis=1)
        pv = jnp.dot(p.astype(BF16), v_aug, preferred_element_type=F32)
        acc_ref[...] = jnp.concatenate([alpha, alpha], axis=1) * acc_ref[...] + pv
        m_ref[...] = m_new
        return carry

    lax.fori_loop(0, nkt, body, 0)
    acc = acc_ref[...]
    o = acc[:, :HEAD_DIM] / jnp.maximum(acc[:, HEAD_DIM:], 1e-30)
    for g in range(group):
        o_ref[:, g * HEAD_DIM:(g + 1) * HEAD_DIM] = o[g * QBLOCK:(g + 1) * QBLOCK].astype(BF16)


def dsa_attend(nkt, q, k, v, bias, n_heads):
    nb, lk, _ = k.shape
    nq = bias.shape[0] // nb
    group = n_heads // N_KV
    gw = group * HEAD_DIM
    qspec = pl.BlockSpec((QBLOCK, gw), lambda b, h, q, *_: (b * nq + q, h))
    kvspec = pl.BlockSpec((None, lk, HEAD_DIM), lambda b, h, q, *_: (b, 0, h))
    grid_spec = pltpu.PrefetchScalarGridSpec(
        num_scalar_prefetch=1,
        grid=(nb, N_KV, nq),
        in_specs=[qspec, kvspec, kvspec,
                  pl.BlockSpec((None, lk, LANES), lambda b, h, q, *_: (b * nq + q, 0, 0))],
        out_specs=qspec,
        scratch_shapes=[pltpu.VMEM((group * QBLOCK, LANES), F32),
                        pltpu.VMEM((group * QBLOCK, 2 * HEAD_DIM), F32)],
    )
    return pl.pallas_call(
        functools.partial(_masked_attn_kernel, group=group),
        grid_spec=grid_spec,
        out_shape=jax.ShapeDtypeStruct(q.shape, BF16),
        compiler_params=_cparams("parallel", "parallel", "arbitrary"),
        name="dsa_attend",
    )(nkt, q, k, v, bias)


def _router_kernel(l_ref, b_ref, id_ref, gate_ref):
    x = l_ref[...] + b_ref[...]
    lane = lax.broadcasted_iota(I32, x.shape, 1)
    big = jnp.int32(2 ** 30)
    ninf = -jnp.inf

    def first_max(vals):
        m = jnp.max(vals, axis=-1, keepdims=True)
        return m, jnp.min(jnp.where(vals == m, lane, big), axis=-1, keepdims=True)

    glog = jnp.where(lane < N_GROUPS, x, ninf)
    gmax, gsel = first_max(glog)
    gprob = 1.0 / jnp.sum(jnp.exp(glog - gmax), axis=-1, keepdims=True)
    in_group = (lane >= N_GROUPS) & ((lane - N_GROUPS) // EXP_PER_GROUP == gsel) & (lane < N_GROUPS + N_EXPERTS)
    e1 = jnp.where(in_group, x, ninf)
    v1, i1 = first_max(e1)
    v2, i2 = first_max(jnp.where(lane == i1, ninf, e1))
    t = jnp.exp(v2 - v1)
    g1 = gprob / (1.0 + t)
    g2 = gprob * t / (1.0 + t)
    id_ref[...] = jnp.where(lane == 0, i1 - N_GROUPS, jnp.where(lane == 1, i2 - N_GROUPS, 0))
    gate_ref[...] = jnp.where(lane == 0, g1, jnp.where(lane == 1, g2, 0.0))


def router(logits, bias):
    t = logits.shape[0]
    tr = _pick_tile(t, 256, 8)
    row = pl.BlockSpec((tr, LANES), lambda i: (i, 0))
    return pl.pallas_call(
        _router_kernel,
        grid=(t // tr,),
        in_specs=[row, pl.BlockSpec((1, LANES), lambda i: (0, 0))],
        out_specs=[row, row],
        out_shape=[jax.ShapeDtypeStruct((t, LANES), I32), jax.ShapeDtypeStruct((t, LANES), F32)],
        compiler_params=_cparams("parallel"),
        name="router",
    )(logits, bias)


def _row_copy(src_hbm, row, dst_ref, r, sem):
    return pltpu.make_async_copy(src_hbm.at[pl.ds(row, 1), :], dst_ref.at[pl.ds(r, 1), :], sem)


def _gather_kernel(idx_ref, nused_ref, x_hbm, o_ref, sem, *, rows):
    i = pl.program_id(0)

    @pl.when(i < nused_ref[0])
    def _():
        base = i * rows

        def start(r, carry):
            _row_copy(x_hbm, idx_ref[base + r], o_ref, r, sem).start()
            return carry

        def wait(r, carry):
            _row_copy(x_hbm, 0, o_ref, r, sem).wait()
            return carry

        lax.fori_loop(0, rows, start, 0)
        lax.fori_loop(0, rows, wait, 0)

    @pl.when(i >= nused_ref[0])
    def _():
        o_ref[...] = jnp.zeros(o_ref.shape, o_ref.dtype)


def gather_rows(idx, n_used, x, n_blocks):
    d = x.shape[1]
    grid_spec = pltpu.PrefetchScalarGridSpec(
        num_scalar_prefetch=2,
        grid=(n_blocks,),
        in_specs=[pl.BlockSpec(memory_space=pl.ANY)],
        out_specs=pl.BlockSpec((MOE_ROWS, d), lambda i, *_: (i, 0)),
        scratch_shapes=[pltpu.SemaphoreType.DMA(())],
    )
    return pl.pallas_call(
        functools.partial(_gather_kernel, rows=MOE_ROWS),
        grid_spec=grid_spec,
        out_shape=jax.ShapeDtypeStruct((n_blocks * MOE_ROWS, d), x.dtype),
        compiler_params=_cparams("arbitrary"),
        name="moe_dispatch",
    )(idx, n_used, x)


def _experts_kernel(bexp_ref, nused_ref, x_ref, g_ref, wg_ref, wu_ref, wd_ref, o_ref):
    @pl.when(pl.program_id(0) < nused_ref[0])
    def _():
        x = x_ref[...].astype(BF16)
        a = jnp.dot(x, wg_ref[...], preferred_element_type=F32)
        b = jnp.dot(x, wu_ref[...], preferred_element_type=F32)
        hid = (a * (1.0 / (1.0 + jnp.exp(-a)))) * b
        y = jnp.dot(hid.astype(BF16), wd_ref[...], preferred_element_type=F32)
        o_ref[...] = y * g_ref[...]

    @pl.when(pl.program_id(0) >= nused_ref[0])
    def _():
        o_ref[...] = jnp.zeros(o_ref.shape, o_ref.dtype)


def grouped_experts(blk_exp, n_used, x_sorted, gate_sorted, w_gate, w_up, w_down):
    p, d = x_sorted.shape
    f = w_gate.shape[2]
    n_blocks = p // MOE_ROWS
    blk = lambda i, be, nu: jnp.minimum(i, nu[0] - 1)
    rows = lambda w: pl.BlockSpec((MOE_ROWS, w), lambda i, be, nu: (blk(i, be, nu), 0))
    wspec = lambda a, b: pl.BlockSpec((None, a, b), lambda i, be, nu: (be[blk(i, be, nu)], 0, 0))
    grid_spec = pltpu.PrefetchScalarGridSpec(
        num_scalar_prefetch=2,
        grid=(n_blocks,),
        in_specs=[rows(d), rows(1), wspec(d, f), wspec(d, f), wspec(f, d)],
        out_specs=pl.BlockSpec((MOE_ROWS, d), lambda i, be, nu: (i, 0)),
    )
    return pl.pallas_call(
        _experts_kernel,
        grid_spec=grid_spec,
        out_shape=jax.ShapeDtypeStruct((p, d), F32),
        compiler_params=_cparams("arbitrary"),
        name="moe_experts",
    )(blk_exp, n_used, x_sorted, gate_sorted, w_gate, w_up, w_down)


def _combine_kernel(p0_ref, p1_ref, h_ref, y_hbm, o_ref, buf_ref, sem, *, rows):
    base = pl.program_id(0) * rows

    def start(r, carry):
        _row_copy(y_hbm, p0_ref[base + r], buf_ref.at[0], r, sem).start()
        _row_copy(y_hbm, p1_ref[base + r], buf_ref.at[1], r, sem).start()
        return carry

    def wait(r, carry):
        _row_copy(y_hbm, 0, buf_ref.at[0], r, sem).wait()
        _row_copy(y_hbm, 0, buf_ref.at[1], r, sem).wait()
        return carry

    lax.fori_loop(0, rows, start, 0)
    lax.fori_loop(0, rows, wait, 0)
    o_ref[...] = h_ref[...] + (buf_ref[0] + buf_ref[1])


def combine(pos0, pos1, h, y_sorted):
    t, d = h.shape
    rows = _pick_tile(t, 128, 8)
    row = pl.BlockSpec((rows, d), lambda i, *_: (i, 0))
    grid_spec = pltpu.PrefetchScalarGridSpec(
        num_scalar_prefetch=2,
        grid=(t // rows,),
        in_specs=[row, pl.BlockSpec(memory_space=pl.ANY)],
        out_specs=row,
        scratch_shapes=[pltpu.VMEM((2, rows, d), F32), pltpu.SemaphoreType.DMA(())],
    )
    return pl.pallas_call(
        functools.partial(_combine_kernel, rows=rows),
        grid_spec=grid_spec,
        out_shape=jax.ShapeDtypeStruct((t, d), F32),
        compiler_params=_cparams("arbitrary"),
        name="moe_combine",
    )(pos0, pos1, h, y_sorted)


def _dispatch_plan(expert, n_tokens):
    a = expert.shape[0]
    tok = jnp.arange(a, dtype=I32) // 2
    order = jnp.argsort(expert, stable=True)
    e_s, tok_s = expert[order], tok[order]
    counts = jnp.zeros((N_EXPERTS,), I32).at[expert].add(1)
    start = jnp.cumsum(counts) - counts
    pcounts = (counts + MOE_ROWS - 1) // MOE_ROWS * MOE_ROWS
    pend = jnp.cumsum(pcounts)
    pstart = pend - pcounts
    dest = (pstart[e_s] + (jnp.arange(a, dtype=I32) - start[e_s])).astype(I32)
    n_blocks = (a + MOE_ROWS - 1) // MOE_ROWS + N_EXPERTS
    p = n_blocks * MOE_ROWS
    tok_buf = jnp.zeros((p,), I32).at[dest].set(tok_s)
    slot = jnp.zeros((a,), I32).at[order].set(dest)
    blk_exp = jnp.minimum(jnp.searchsorted(pend, jnp.arange(n_blocks, dtype=I32) * MOE_ROWS, side='right'),
                          N_EXPERTS - 1).astype(I32)
    n_used = (pend[-1] // MOE_ROWS).astype(I32).reshape(1)
    return tok_buf, slot, blk_exp, n_used, n_blocks


def moe_layer(h, norm_g, w_group, b_group, w_expert, b_expert, w_gate, w_up, w_down):
    t, d = h.shape
    xn_f, xn_b = rmsnorm(h, norm_g, (F32, BF16))
    n_logit = N_GROUPS + N_EXPERTS
    w_r = jnp.concatenate([w_group, w_expert, jnp.zeros((d, LANES - n_logit), F32)], axis=1).astype(BF16)
    b_r = jnp.concatenate([b_group.astype(F32), b_expert.astype(F32),
                           jnp.zeros((LANES - n_logit,), F32)]).reshape(1, LANES)
    ids, gates = router(matmul(xn_b, w_r, name="router_logits"), b_r)
    expert = ids[:, :2].reshape(-1)
    gate = gates[:, :2].reshape(-1)
    tok_buf, slot, blk_exp, n_used, n_blocks = _dispatch_plan(expert, t)
    gate_sorted = jnp.zeros((n_blocks * MOE_ROWS, 1), F32).at[slot, 0].set(gate)
    x_sorted = gather_rows(tok_buf, n_used, xn_f, n_blocks)
    y_sorted = grouped_experts(blk_exp, n_used, x_sorted, gate_sorted,
                               w_gate.astype(BF16), w_up.astype(BF16), w_down.astype(BF16))
    slot2 = slot.reshape(t, 2)
    return combine(slot2[:, 0], slot2[:, 1], h, y_sorted)


def _query_blocks(x, n_blocks, width):
    return x.reshape(n_blocks, QBLOCK, IDX_HEADS, width).transpose(0, 2, 1, 3).reshape(
        n_blocks, IDX_HEADS * QBLOCK, width)


def kernel(x_prompt, x_sample, cache_a_k, cache_a_v, cache_b_k, cache_b_v, cache_b_kidx, attn_norm, ffn_norm, a_w_qkv, a_w_o, a_q_norm, a_k_norm, a_sinks, b_w_qkv, b_w_o, b_q_norm, b_k_norm, b_kidx_norm, moe_w_group, moe_b_group, moe_w_expert, moe_b_expert, moe_w_gate, moe_w_up, moe_w_down):
    nb, seq, d = x_prompt.shape
    db, ds, _ = x_sample.shape
    depth = attn_norm.shape[0]
    n_heads = d // HEAD_DIM
    qc = n_heads * HEAD_DIM
    kvc = N_KV * HEAD_DIM
    tp, ts = nb * seq, db * ds
    past = cache_b_k.shape[2]
    nqb = seq // QBLOCK

    h = jnp.concatenate([x_prompt.reshape(tp, d), x_sample.reshape(ts, d)], axis=0)
    pos = jnp.concatenate([jnp.tile(jnp.arange(seq), nb), jnp.tile(past + jnp.arange(ds), db)])
    head_tables = rope_tables(pos, ROT_DIM, HEAD_DIM)
    idx_tables = rope_tables(pos, IDX_ROT, IDX_DIM)

    outs_a = {"kp": [], "vp": [], "ks": [], "vs": []}
    outs_b = {"kp": [], "vp": [], "ip": [], "ks": [], "vs": [], "is": []}
    for i in range(depth):
        j = i // 2
        xn = rmsnorm(h, attn_norm[i], (BF16,))[0]
        if i % 2 == 0:
            proj = matmul(xn, a_w_qkv[j].astype(BF16), name="qkv_proj")
            q_b, k_f, k_b, v_f, v_b = qkv_post(proj, a_q_norm[j], a_k_norm[j], head_tables, n_heads)
            o_p = swa_prompt(q_b, k_b, v_b, a_sinks[j], nb, seq, n_heads)
            k_all = jnp.concatenate([cache_a_k[j].reshape(db, WINDOW, kvc).astype(F32),
                                     k_f[tp:].reshape(db, ds, kvc)], axis=1)
            v_all = jnp.concatenate([cache_a_v[j].reshape(db, WINDOW, kvc).astype(F32),
                                     v_f[tp:].reshape(db, ds, kvc)], axis=1)
            o_s = swa_sample(q_b, k_all.astype(BF16), v_all.astype(BF16), a_sinks[j], tp, db, ds, n_heads)
            outs_a["kp"].append(k_f[:tp].reshape(nb, seq, N_KV, HEAD_DIM)[:, -WINDOW:])
            outs_a["vp"].append(v_f[:tp].reshape(nb, seq, N_KV, HEAD_DIM)[:, -WINDOW:])
            outs_a["ks"].append(k_all[:, -WINDOW:].reshape(db, WINDOW, N_KV, HEAD_DIM))
            outs_a["vs"].append(v_all[:, -WINDOW:].reshape(db, WINDOW, N_KV, HEAD_DIM))
            w_o = a_w_o[j]
        else:
            n_qkv = qc + 2 * kvc
            n_rest = b_w_qkv.shape[2] - n_qkv
            rest_w = IDX_HEADS * IDX_DIM + LANES
            proj = matmul(xn, b_w_qkv[j][:, :n_qkv].astype(BF16), name="qkv_proj")
            w_rest = jnp.pad(b_w_qkv[j][:, n_qkv:], ((0, 0), (0, rest_w - n_rest))).astype(BF16)
            rest = matmul(xn, w_rest, tn_target=384, name="idx_proj")
            q_b, k_f, k_b, v_f, v_b = qkv_post(proj, b_q_norm[j], b_k_norm[j], head_tables, n_heads)
            qi_b, tail = idx_post(rest, b_kidx_norm[j], idx_tables)
            kidx_f = tail[:, :IDX_DIM]
            w_idx = tail[:, IDX_DIM:IDX_DIM + IDX_HEADS]

            blk_q = jnp.arange(nqb, dtype=I32)
            nkt_p = jnp.tile(((blk_q + 1) * QBLOCK + KEY_TILE - 1) // KEY_TILE, nb).astype(I32)
            qrow = blk_q[:, None] * QBLOCK + jnp.arange(QBLOCK, dtype=I32)[None, :]
            lim_p = jnp.tile(((qrow // CHUNK + 1) * CHUNK)[:, None, :], (nb, 1, 1)).astype(I32)
            qi_p = _query_blocks(qi_b[:tp], nb * nqb, IDX_DIM)
            w_p = _query_blocks(w_idx[:tp], nb * nqb, 1).reshape(nb * nqb, 1, IDX_HEADS * QBLOCK)
            bias_p = dsa_select(nkt_p, qi_p, w_p, lim_p, kidx_f[:tp].astype(BF16).reshape(nb, seq, IDX_DIM),
                                min(TOPK_MAX, seq // 4))
            o_p = dsa_attend(nkt_p, q_b[:tp], k_b[:tp].reshape(nb, seq, kvc), v_b[:tp].reshape(nb, seq, kvc),
                             bias_p, n_heads)

            n_keys = past + ds
            lk = (n_keys + KEY_TILE - 1) // KEY_TILE * KEY_TILE
            pad_k = lambda new, cache: jnp.pad(
                jnp.concatenate([cache.reshape(db, past, -1).astype(F32), new.reshape(db, ds, -1)], axis=1),
                ((0, 0), (0, lk - n_keys), (0, 0))).astype(BF16)
            pad_q = lambda x: jnp.pad(x.reshape(db, ds, -1), ((0, 0), (0, QBLOCK - ds), (0, 0))).reshape(
                db * QBLOCK, -1)
            nkt_s = jnp.full((db,), lk // KEY_TILE, I32)
            lim_s = jnp.tile(jnp.where(jnp.arange(QBLOCK) < ds, n_keys, 0).astype(I32)[None, None, :], (db, 1, 1))
            qi_s = _query_blocks(pad_q(qi_b[tp:]), db, IDX_DIM)
            w_s = _query_blocks(pad_q(w_idx[tp:]), db, 1).reshape(db, 1, IDX_HEADS * QBLOCK)
            bias_s = dsa_select(nkt_s, qi_s, w_s, lim_s, pad_k(kidx_f[tp:], cache_b_kidx[j]),
                                min(TOPK_MAX, n_keys // 4))
            o_s = dsa_attend(nkt_s, pad_q(q_b[tp:]), pad_k(k_f[tp:], cache_b_k[j]), pad_k(v_f[tp:], cache_b_v[j]),
                             bias_s, n_heads)
            o_s = o_s.reshape(db, QBLOCK, qc)[:, :ds].reshape(ts, qc)

            outs_b["kp"].append(k_f[:tp].reshape(nb, seq, N_KV, HEAD_DIM))
            outs_b["vp"].append(v_f[:tp].reshape(nb, seq, N_KV, HEAD_DIM))
            outs_b["ip"].append(kidx_f[:tp].reshape(nb, seq, IDX_DIM))
            outs_b["ks"].append(k_f[tp:].reshape(db, ds, N_KV, HEAD_DIM))
            outs_b["vs"].append(v_f[tp:].reshape(db, ds, N_KV, HEAD_DIM))
            outs_b["is"].append(kidx_f[tp:].reshape(db, ds, IDX_DIM))
            w_o = b_w_o[j]
        h = matmul(jnp.concatenate([o_p, o_s], axis=0), w_o.astype(BF16), res=h, name="out_proj")
        h = moe_layer(h, ffn_norm[i], moe_w_group[i], moe_b_group[i], moe_w_expert[i], moe_b_expert[i],
                      moe_w_gate[i], moe_w_up[i], moe_w_down[i])

    return (h[:tp].reshape(nb, seq, d), h[tp:].reshape(db, ds, d),
            jnp.stack(outs_a["kp"]), jnp.stack(outs_a["vp"]),
            jnp.stack(outs_b["kp"]), jnp.stack(outs_b["vp"]), jnp.stack(outs_b["ip"]),
            jnp.stack(outs_a["ks"]), jnp.stack(outs_a["vs"]),
            jnp.stack(outs_b["ks"]), jnp.stack(outs_b["vs"]), jnp.stack(outs_b["is"]))
```

```python
import functools
import math

import jax
import jax.numpy as jnp
from jax import lax
from jax.experimental import pallas as pl
from jax.experimental.pallas import tpu as pltpu

F32 = jnp.float32
BF16 = jnp.bfloat16
I32 = jnp.int32
I16 = jnp.int16

LANES = 128
CHUNK = 64
HEAD_DIM = 128
N_KV = 8
WINDOW = 128
ROT_DIM = HEAD_DIM // 4
ROPE_THETA = 500000.0
IDX_HEADS = 16
IDX_DIM = 64
IDX_ROT = IDX_DIM // 4
TOPK_MAX = 256
QBLOCK = 128
N_GROUPS = 8
EXP_PER_GROUP = 8
N_EXPERTS = N_GROUPS * EXP_PER_GROUP
RMS_EPS = 1e-6
NEG = -1e30
INT_MIN = -2 ** 31
KEY_TILE = 512
ATT_TILE = 512
LOG2E = 1.4426950408889634
MOE_ROWS = 256
VMEM_LIMIT = 56 * 1024 * 1024

_NT = (((1,), (1,)), ((), ()))


def _cparams(*sem):
    return pltpu.CompilerParams(dimension_semantics=sem, vmem_limit_bytes=VMEM_LIMIT)


def _pick_tile(n, target, mult):
    best = None
    t = mult
    while t <= min(n, target):
        if n % t == 0:
            best = t
        t += mult
    return best if best is not None else n


def _rmsnorm_kernel(x_ref, g_ref, *o_refs):
    x = x_ref[...]
    y = x * lax.rsqrt(jnp.mean(x * x, axis=-1, keepdims=True) + RMS_EPS) * g_ref[...]
    for o_ref in o_refs:
        o_ref[...] = y.astype(o_ref.dtype)


def rmsnorm(x, g, out_dtypes):
    t, d = x.shape
    tr = _pick_tile(t, 256, 16)
    outs = pl.pallas_call(
        _rmsnorm_kernel,
        grid=(t // tr,),
        in_specs=[pl.BlockSpec((tr, d), lambda i: (i, 0)),
                  pl.BlockSpec((1, d), lambda i: (0, 0))],
        out_specs=[pl.BlockSpec((tr, d), lambda i: (i, 0)) for _ in out_dtypes],
        out_shape=[jax.ShapeDtypeStruct((t, d), dt) for dt in out_dtypes],
        compiler_params=_cparams("parallel"),
        name="rmsnorm",
    )(x, g.reshape(1, d).astype(F32))
    return outs


def _mm_kernel(a_ref, b_ref, o_ref):
    o_ref[...] = jnp.dot(a_ref[...], b_ref[...], preferred_element_type=F32).astype(o_ref.dtype)


def _mm_res_kernel(a_ref, b_ref, r_ref, o_ref):
    o_ref[...] = r_ref[...] + jnp.dot(a_ref[...], b_ref[...], preferred_element_type=F32)


def matmul(a, b, res=None, tm_target=1280, tn_target=512, name="matmul"):
    m, k = a.shape
    _, n = b.shape
    tm = _pick_tile(m, tm_target, 16)
    tn = _pick_tile(n, tn_target, LANES)
    in_specs = [pl.BlockSpec((tm, k), lambda i, j: (i, 0)),
                pl.BlockSpec((k, tn), lambda i, j: (0, j))]
    args = [a, b]
    if res is not None:
        in_specs.append(pl.BlockSpec((tm, tn), lambda i, j: (i, j)))
        args.append(res)
    return pl.pallas_call(
        _mm_kernel if res is None else _mm_res_kernel,
        grid=(m // tm, n // tn),
        in_specs=in_specs,
        out_specs=pl.BlockSpec((tm, tn), lambda i, j: (i, j)),
        out_shape=jax.ShapeDtypeStruct((m, n), F32),
        compiler_params=_cparams("parallel", "arbitrary"),
        name=name,
    )(*args)


def rope_tables(pos, rot_dim, period):
    half = rot_dim // 2
    inv_freq = 1.0 / (ROPE_THETA ** (jnp.arange(half, dtype=F32) * (2.0 / rot_dim)))
    ang = pos.astype(F32)[:, None] * inv_freq[None, :]
    cos, sin = jnp.cos(ang), jnp.sin(ang)
    t = pos.shape[0]
    rest = period - rot_dim
    c = jnp.concatenate([cos, cos, jnp.ones((t, rest), F32)], axis=1)
    s1 = jnp.concatenate([-sin, jnp.zeros((t, half + rest), F32)], axis=1)
    s2 = jnp.concatenate([jnp.zeros((t, half), F32), sin, jnp.zeros((t, rest), F32)], axis=1)
    rep = LANES // period
    return tuple(jnp.tile(x, (1, rep)) for x in (c, s1, s2))


def _rope(y, c, s1, s2, half):
    return y * c + pltpu.roll(y, LANES - half, 1) * s1 + pltpu.roll(y, half, 1) * s2


def _qkv_post_kernel(p_ref, qg_ref, kg_ref, c_ref, s1_ref, s2_ref,
                     q_ref, kf_ref, kb_ref, vf_ref, vb_ref, *, n_heads, q_scale):
    c, s1, s2 = c_ref[...], s1_ref[...], s2_ref[...]

    def normrope(x, g):
        y = x * lax.rsqrt(jnp.mean(x * x, axis=-1, keepdims=True) + RMS_EPS) * g
        return _rope(y, c, s1, s2, ROT_DIM // 2)

    qg, kg = qg_ref[...], kg_ref[...]
    for h in range(n_heads):
        sl = slice(h * HEAD_DIM, (h + 1) * HEAD_DIM)
        q_ref[:, sl] = (normrope(p_ref[:, sl], qg) * q_scale).astype(BF16)
    for h in range(N_KV):
        sl = slice(h * HEAD_DIM, (h + 1) * HEAD_DIM)
        k = normrope(p_ref[:, (n_heads + h) * HEAD_DIM:(n_heads + h + 1) * HEAD_DIM], kg)
        kf_ref[:, h, :] = k
        kb_ref[:, sl] = k.astype(BF16)
        v = p_ref[:, (n_heads + N_KV + h) * HEAD_DIM:(n_heads + N_KV + h + 1) * HEAD_DIM]
        vf_ref[:, h, :] = v
        vb_ref[:, sl] = v.astype(BF16)


def qkv_post(proj, q_gain, k_gain, tables, n_heads, q_scale):
    t, ncol = proj.shape
    tr = _pick_tile(t, 256, 16)
    kvc = N_KV * HEAD_DIM
    qc = n_heads * HEAD_DIM
    row = lambda w: pl.BlockSpec((tr, w), lambda i: (i, 0))
    row3 = pl.BlockSpec((tr, N_KV, HEAD_DIM), lambda i: (i, 0, 0))
    vec = pl.BlockSpec((1, HEAD_DIM), lambda i: (0, 0))
    cache = jax.ShapeDtypeStruct((t, N_KV, HEAD_DIM), F32)
    return pl.pallas_call(
        functools.partial(_qkv_post_kernel, n_heads=n_heads, q_scale=q_scale),
        grid=(t // tr,),
        in_specs=[row(ncol), vec, vec, row(LANES), row(LANES), row(LANES)],
        out_specs=[row(qc), row3, row(kvc), row3, row(kvc)],
        out_shape=[jax.ShapeDtypeStruct((t, qc), BF16),
                   cache, jax.ShapeDtypeStruct((t, kvc), BF16),
                   cache, jax.ShapeDtypeStruct((t, kvc), BF16)],
        compiler_params=_cparams("parallel"),
        name="qkv_post",
    )(proj, q_gain.reshape(1, HEAD_DIM).astype(F32), k_gain.reshape(1, HEAD_DIM).astype(F32), *tables)


IDX_W_SCALE = (IDX_HEADS ** -0.5) * (IDX_DIM ** -0.5)


def _idx_post_kernel(r_ref, g_ref, c_ref, s1_ref, s2_ref, qi_ref, tail_ref, kb_ref):
    c, s1, s2 = c_ref[...], s1_ref[...], s2_ref[...]
    nq = IDX_HEADS * IDX_DIM
    for j in range(nq // LANES):
        sl = slice(j * LANES, (j + 1) * LANES)
        qi_ref[:, sl] = _rope(r_ref[:, sl], c, s1, s2, IDX_ROT // 2).astype(BF16)
    t = r_ref[:, nq:nq + LANES]
    lane = lax.broadcasted_iota(I32, t.shape, 1)
    is_k = lane < IDX_DIM
    ms = jnp.sum(jnp.where(is_k, t * t, 0.0), axis=-1, keepdims=True) * (1.0 / IDX_DIM)
    kn = t * lax.rsqrt(ms + RMS_EPS) * g_ref[...]
    tail = jnp.where(is_k, _rope(kn, c, s1, s2, IDX_ROT // 2), t * IDX_W_SCALE)
    tail_ref[...] = tail
    kb_ref[...] = tail[:, :IDX_DIM].astype(BF16)


def idx_post(rest, kidx_gain, tables):
    t, ncol = rest.shape
    tr = _pick_tile(t, 256, 16)
    nq = IDX_HEADS * IDX_DIM
    row = lambda w: pl.BlockSpec((tr, w), lambda i: (i, 0))
    g = jnp.concatenate([kidx_gain.astype(F32), jnp.zeros((LANES - IDX_DIM,), F32)]).reshape(1, LANES)
    return pl.pallas_call(
        _idx_post_kernel,
        grid=(t // tr,),
        in_specs=[row(ncol), pl.BlockSpec((1, LANES), lambda i: (0, 0)), row(LANES), row(LANES), row(LANES)],
        out_specs=[row(nq), row(LANES), row(IDX_DIM)],
        out_shape=[jax.ShapeDtypeStruct((t, nq), BF16), jax.ShapeDtypeStruct((t, LANES), F32),
                   jax.ShapeDtypeStruct((t, IDX_DIM), BF16)],
        compiler_params=_cparams("parallel"),
        name="idx_post",
    )(rest, g, *tables)


def _sink_attn(q, k, v, bias, sink):
    s = lax.dot_general(q, k, _NT, preferred_element_type=F32)
    if bias is not None:
        s = s + bias
    m = jnp.maximum(jnp.max(s, axis=-1, keepdims=True), sink)
    p = jnp.exp(s - m)
    denom = jnp.sum(p, axis=-1, keepdims=True) + jnp.exp(sink - m)
    o = jnp.dot(p.astype(BF16), v, preferred_element_type=F32)
    return o / denom


def _swa_prompt_kernel(sink_ref, q_ref, kp_ref, kc_ref, vp_ref, vc_ref, o_ref, *, group):
    i = pl.program_id(1)
    rows, cols = QBLOCK, 2 * QBLOCK
    qc = lax.broadcasted_iota(I32, (rows, cols), 0) // CHUNK
    kc = lax.broadcasted_iota(I32, (rows, cols), 1) // CHUNK
    first_valid = jnp.where(i > 0, 0, QBLOCK // CHUNK)
    vis = (kc >= qc) & (kc <= qc + WINDOW // CHUNK) & (kc >= first_valid)
    bias = jnp.where(vis, 0.0, NEG)
    for kv in range(N_KV):
        sl = slice(kv * HEAD_DIM, (kv + 1) * HEAD_DIM)
        k = jnp.concatenate([kp_ref[:, sl], kc_ref[:, sl]], axis=0)
        v = jnp.concatenate([vp_ref[:, sl], vc_ref[:, sl]], axis=0)
        for g in range(group):
            h = kv * group + g
            hs = slice(h * HEAD_DIM, (h + 1) * HEAD_DIM)
            o_ref[:, hs] = _sink_attn(q_ref[:, hs], k, v, bias, sink_ref[h]).astype(BF16)


def swa_prompt(q, k, v, sinks, n_batch, seq, n_heads):
    nqb = seq // QBLOCK
    qc = n_heads * HEAD_DIM
    kvc = N_KV * HEAD_DIM
    cur = lambda w: pl.BlockSpec((QBLOCK, w), lambda b, i: (b * nqb + i, 0))
    prev = lambda w: pl.BlockSpec((QBLOCK, w), lambda b, i: (b * nqb + jnp.maximum(i - 1, 0), 0))
    return pl.pallas_call(
        functools.partial(_swa_prompt_kernel, group=n_heads // N_KV),
        grid=(n_batch, nqb),
        in_specs=[pl.BlockSpec(memory_space=pltpu.SMEM),
                  cur(qc), prev(kvc), cur(kvc), prev(kvc), cur(kvc)],
        out_specs=cur(qc),
        out_shape=jax.ShapeDtypeStruct((n_batch * seq, qc), BF16),
        compiler_params=_cparams("parallel", "arbitrary"),
        name="swa_prompt",
    )(sinks.astype(F32), q, k, k, v, v)


def _swa_sample_kernel(sink_ref, q_ref, k_ref, v_ref, o_ref, *, group):
    for kv in range(N_KV):
        sl = slice(kv * HEAD_DIM, (kv + 1) * HEAD_DIM)
        k, v = k_ref[:, sl], v_ref[:, sl]
        for g in range(group):
            h = kv * group + g
            hs = slice(h * HEAD_DIM, (h + 1) * HEAD_DIM)
            o_ref[:, hs] = _sink_attn(q_ref[:, hs], k, v, None, sink_ref[h]).astype(BF16)


def swa_sample(q, k_all, v_all, sinks, row0, n_batch, n_new, n_heads):
    qc = n_heads * HEAD_DIM
    nk = k_all.shape[1]
    kvc = N_KV * HEAD_DIM
    return pl.pallas_call(
        functools.partial(_swa_sample_kernel, group=n_heads // N_KV),
        grid=(n_batch,),
        in_specs=[pl.BlockSpec(memory_space=pltpu.SMEM),
                  pl.BlockSpec((n_new, qc), lambda b: (row0 // n_new + b, 0)),
                  pl.BlockSpec((None, nk, kvc), lambda b: (b, 0, 0)),
                  pl.BlockSpec((None, nk, kvc), lambda b: (b, 0, 0))],
        out_specs=pl.BlockSpec((n_new, qc), lambda b: (b, 0)),
        out_shape=jax.ShapeDtypeStruct((n_batch * n_new, qc), BF16),
        compiler_params=_cparams("parallel"),
        name="swa_sample",
    )(sinks.astype(F32), q, k_all, v_all)


def _select_kernel(nkt_ref, qi_ref, w_ref, lim_ref, kidx_ref, bias_ref, keys_ref, hi_ref, lo_ref, p_ref,
                   *, topk, n_tiles, idx_bits):
    tk = KEY_TILE
    blk = pl.program_id(0) * pl.num_programs(1) + pl.program_id(1)
    nkt = nkt_ref[blk]
    qi = qi_ref[...]
    w = w_ref[...]
    lim = lim_ref[...]
    imin = jnp.int32(INT_MIN)
    half_min = -2 ** 15

    def tile_rows(kt):
        return pl.ds(pl.multiple_of(kt * tk, tk), tk)

    def key_index(kt):
        return kt * tk + lax.broadcasted_iota(I32, (tk, LANES), 0)

    def score_tile(kt, carry):
        d = lax.dot_general(kidx_ref[tile_rows(kt), :], qi, _NT, preferred_element_type=F32)
        r = jnp.maximum(d, 0.0) * w
        acc = r[:, 0:LANES]
        for h in range(1, IDX_HEADS):
            acc = acc + r[:, h * LANES:(h + 1) * LANES]
        bits = pltpu.bitcast(acc, I32)
        key = bits ^ ((bits >> 31) & jnp.int32(0x7FFFFFFF))
        key = jnp.where(key_index(kt) < lim, key, imin)
        keys_ref[tile_rows(kt), :] = key
        hi_ref[tile_rows(kt), :] = (key >> 16).astype(I16)
        lo_ref[tile_rows(kt), :] = ((key & 0xFFFF) + half_min).astype(I16)
        return carry

    lax.fori_loop(0, nkt, score_tile, 0)

    def count(pred):
        def body(kt, acc):
            c = jnp.where(pred(keys_ref[tile_rows(kt), :], key_index(kt)), 1, 0).astype(I32)
            return acc + jnp.sum(c.reshape(tk // 8, 8, LANES), axis=0)
        acc = lax.fori_loop(0, nkt, body, jnp.zeros((8, LANES), I32))
        return jnp.sum(acc, axis=0, keepdims=True)

    n_pairs = (nkt + 1) // 2

    def pair_rows(j):
        return pl.ds(pl.multiple_of(j * (2 * tk), 2 * tk), 2 * tk)

    @pl.when(nkt < n_tiles)
    def _():
        hi_ref[tile_rows(nkt), :] = jnp.full((tk, LANES), half_min, I16)
        lo_ref[tile_rows(nkt), :] = jnp.full((tk, LANES), half_min, I16)

    def count_ge16(ref, cand):
        c16 = cand.astype(I16)
        pack = 16

        def body(j, acc):
            c = jnp.where(ref[pair_rows(j), :] >= c16, jnp.int16(1), jnp.int16(0))
            parts = [c[r * pack:(r + 1) * pack, :] for r in range(2 * tk // pack)]
            while len(parts) > 1:
                parts = [parts[a] + parts[a + 1] for a in range(0, len(parts), 2)]
            return acc + parts[0]
        acc = lax.fori_loop(0, n_pairs, body, jnp.zeros((pack, LANES), I16))
        return jnp.sum(acc.astype(I32), axis=0, keepdims=True)

    def search16(ref, want):
        def bit(i, v):
            cand = v + lax.shift_left(jnp.int32(1), 15 - i)
            return jnp.where(count_ge16(ref, cand) >= want, cand, v)
        return lax.fori_loop(0, 16, bit, jnp.full((1, LANES), half_min, I32))

    tau_hi = search16(hi_ref, topk)
    n_above = count_ge16(hi_ref, tau_hi + 1)
    n_above = jnp.where(tau_hi == 2 ** 15 - 1, 0, n_above)
    tau_hi16 = tau_hi.astype(I16)

    def mask_low(kt, carry):
        rows = tile_rows(kt)
        lo_ref[rows, :] = jnp.where(hi_ref[rows, :] == tau_hi16, lo_ref[rows, :], jnp.int16(half_min))
        return carry

    lax.fori_loop(0, nkt, mask_low, 0)
    tau_lo = search16(lo_ref, topk - n_above)
    tau = lax.shift_left(tau_hi, 16) | (tau_lo - half_min)

    n_ge = count(lambda kk, _: (kk >= tau) & (kk != imin))
    n_gt = count(lambda kk, _: kk > tau)
    need = topk - n_gt

    p_ref[...] = jnp.full((1, LANES), 2 ** 30, I32)

    @pl.when(jnp.max(n_ge) > topk)
    def _():
        def index_bit(i, p):
            cand = p + lax.shift_left(jnp.int32(1), idx_bits - 1 - i)
            cnt = count(lambda kk, si: (kk == tau) & (si < cand))
            return jnp.where(cnt < need, cand, p)
        p_ref[...] = lax.fori_loop(0, idx_bits, index_bit, jnp.zeros((1, LANES), I32))

    p_last = p_ref[...]

    def write_tile(kt, carry):
        kk = keys_ref[tile_rows(kt), :]
        sel = (kk > tau) | ((kk == tau) & (key_index(kt) <= p_last) & (kk != imin))
        bias_ref[tile_rows(kt), :] = jnp.where(sel, 0.0, NEG).astype(BF16)
        return carry

    lax.fori_loop(0, nkt, write_tile, 0)

    def fill_tile(kt, carry):
        bias_ref[tile_rows(kt), :] = jnp.full((tk, LANES), NEG, BF16)
        return carry

    lax.fori_loop(nkt, n_tiles, fill_tile, 0)


def dsa_select(nkt, qi_blocks, w_blocks, limits, kidx, nb, lk, topk):
    nq = qi_blocks.shape[0] // nb
    n_tiles = lk // KEY_TILE
    idx_bits = max(1, int(lk).bit_length())
    per_block = lambda shape: pl.BlockSpec((None,) + shape, lambda b, q, *_: (b * nq + q, 0, 0))
    grid_spec = pltpu.PrefetchScalarGridSpec(
        num_scalar_prefetch=1,
        grid=(nb, nq),
        in_specs=[per_block((IDX_HEADS * QBLOCK, IDX_DIM)),
                  per_block((1, IDX_HEADS * QBLOCK)),
                  per_block((1, LANES)),
                  pl.BlockSpec((lk, IDX_DIM), lambda b, q, *_: (b, 0))],
        out_specs=per_block((lk, LANES)),
        scratch_shapes=[pltpu.VMEM((lk, LANES), I32), pltpu.VMEM((lk, LANES), I16), pltpu.VMEM((lk, LANES), I16),
                        pltpu.VMEM((1, LANES), I32)],
    )
    return pl.pallas_call(
        functools.partial(_select_kernel, topk=topk, n_tiles=n_tiles, idx_bits=idx_bits),
        grid_spec=grid_spec,
        out_shape=jax.ShapeDtypeStruct((nb * nq, lk, LANES), BF16),
        compiler_params=_cparams("parallel", "arbitrary"),
        name="dsa_select",
    )(nkt, qi_blocks, w_blocks, limits, kidx)


def _masked_attn_kernel(nkt_ref, q_ref, k_ref, v_ref, bias_ref, o_ref, m_ref, acc_ref, s0_ref, s1_ref,
                        *, group, n_tiles):
    tk = ATT_TILE
    blk = pl.program_id(0) * pl.num_programs(2) + pl.program_id(2)
    n_pairs = (nkt_ref[blk] + 1) // 2
    rows = group * QBLOCK
    eye = jnp.where(lax.broadcasted_iota(I32, (QBLOCK, QBLOCK), 0)
                    == lax.broadcasted_iota(I32, (QBLOCK, QBLOCK), 1), 1.0, 0.0).astype(BF16)
    q_aug = jnp.concatenate(
        [jnp.concatenate([q_ref[:, g * HEAD_DIM:(g + 1) * HEAD_DIM], eye], axis=1) for g in range(group)],
        axis=0)
    ones = jnp.ones((tk, HEAD_DIM), BF16)
    m_ref[...] = jnp.full((rows, LANES), NEG, F32)
    acc_ref[...] = jnp.zeros((rows, 2 * HEAD_DIM), F32)

    def tile_rows(kt):
        return pl.ds(pl.multiple_of(jnp.minimum(kt, n_tiles - 1) * tk, tk), tk)

    def scores(kt):
        sl = tile_rows(kt)
        k_aug = jnp.concatenate([k_ref[sl, :], bias_ref[sl, :]], axis=1)
        return lax.dot_general(q_aug, k_aug, _NT, preferred_element_type=F32)

    def update(s_ref, kt):
        s = s_ref[...]
        m_prev = m_ref[...]
        m_new = jnp.maximum(m_prev, jnp.max(s, axis=-1, keepdims=True))
        alpha = jnp.exp2(m_prev - m_new)
        p = jnp.exp2(s - jnp.concatenate([m_new] * (tk // LANES), axis=1))
        v_aug = jnp.concatenate([v_ref[tile_rows(kt), :], ones], axis=1)
        pv = jnp.dot(p.astype(BF16), v_aug, preferred_element_type=F32)
        acc_ref[...] = jnp.concatenate([alpha, alpha], axis=1) * acc_ref[...] + pv
        m_ref[...] = m_new

    s0_ref[...] = scores(0)

    def body(u, carry):
        s1_ref[...] = scores(2 * u + 1)
        update(s0_ref, 2 * u)
        s0_ref[...] = scores(2 * u + 2)
        update(s1_ref, 2 * u + 1)
        return carry

    lax.fori_loop(0, n_pairs, body, 0)
    acc = acc_ref[...]
    o = acc[:, :HEAD_DIM] / jnp.maximum(acc[:, HEAD_DIM:], 1e-30)
    for g in range(group):
        o_ref[:, g * HEAD_DIM:(g + 1) * HEAD_DIM] = o[g * QBLOCK:(g + 1) * QBLOCK].astype(BF16)


def dsa_attend(nkt, q, k, v, bias, nb, n_heads):
    lk = bias.shape[1]
    nq = bias.shape[0] // nb
    group = n_heads // N_KV
    gw = group * HEAD_DIM
    rows = group * QBLOCK
    n_tiles = lk // ATT_TILE
    assert lk % (2 * ATT_TILE) == 0, "the tile-pair loop needs an even number of key tiles"
    qspec = pl.BlockSpec((QBLOCK, gw), lambda b, h, q, *_: (b * nq + q, h))
    kvspec = pl.BlockSpec((lk, HEAD_DIM), lambda b, h, q, *_: (b, h))
    grid_spec = pltpu.PrefetchScalarGridSpec(
        num_scalar_prefetch=1,
        grid=(nb, N_KV, nq),
        in_specs=[qspec, kvspec, kvspec,
                  pl.BlockSpec((None, lk, LANES), lambda b, h, q, *_: (b * nq + q, 0, 0))],
        out_specs=qspec,
        scratch_shapes=[pltpu.VMEM((rows, LANES), F32),
                        pltpu.VMEM((rows, 2 * HEAD_DIM), F32),
                        pltpu.VMEM((rows, ATT_TILE), F32),
                        pltpu.VMEM((rows, ATT_TILE), F32)],
    )
    return pl.pallas_call(
        functools.partial(_masked_attn_kernel, group=group, n_tiles=n_tiles),
        grid_spec=grid_spec,
        out_shape=jax.ShapeDtypeStruct((nb * nq * QBLOCK, q.shape[1]), BF16),
        compiler_params=_cparams("parallel", "parallel", "arbitrary"),
        name="dsa_attend",
    )(nkt, q, k, v, bias)


def _router_kernel(l_ref, b_ref, id_ref, gate_ref):
    x = l_ref[...] + b_ref[...]
    lane = lax.broadcasted_iota(I32, x.shape, 1)
    big = jnp.int32(2 ** 30)
    ninf = -jnp.inf

    def first_max(vals):
        m = jnp.max(vals, axis=-1, keepdims=True)
        return m, jnp.min(jnp.where(vals == m, lane, big), axis=-1, keepdims=True)

    glog = jnp.where(lane < N_GROUPS, x, ninf)
    gmax, gsel = first_max(glog)
    gprob = 1.0 / jnp.sum(jnp.exp(glog - gmax), axis=-1, keepdims=True)
    in_group = (lane >= N_GROUPS) & ((lane - N_GROUPS) // EXP_PER_GROUP == gsel) & (lane < N_GROUPS + N_EXPERTS)
    e1 = jnp.where(in_group, x, ninf)
    v1, i1 = first_max(e1)
    v2, i2 = first_max(jnp.where(lane == i1, ninf, e1))
    t = jnp.exp(v2 - v1)
    g1 = gprob / (1.0 + t)
    g2 = gprob * t / (1.0 + t)
    id_ref[...] = jnp.where(lane == 0, i1 - N_GROUPS, jnp.where(lane == 1, i2 - N_GROUPS, 0))
    gate_ref[...] = jnp.where(lane == 0, g1, jnp.where(lane == 1, g2, 0.0))


def router(logits, bias):
    t = logits.shape[0]
    tr = _pick_tile(t, 256, 8)
    row = pl.BlockSpec((tr, LANES), lambda i: (i, 0))
    return pl.pallas_call(
        _router_kernel,
        grid=(t // tr,),
        in_specs=[row, pl.BlockSpec((1, LANES), lambda i: (0, 0))],
        out_specs=[row, row],
        out_shape=[jax.ShapeDtypeStruct((t, LANES), I32), jax.ShapeDtypeStruct((t, LANES), F32)],
        compiler_params=_cparams("parallel"),
        name="router",
    )(logits, bias)


DMA_UNROLL = 8


def _row_copy(src_hbm, row, dst_ref, r, sem):
    return pltpu.make_async_copy(src_hbm.at[pl.ds(row, 1), :], dst_ref.at[pl.ds(r, 1), :], sem)


def _start_row_gather(src_hbm, row_of, dst_ref, sem, n_rows):
    def trip(j, carry):
        for u in range(DMA_UNROLL):
            r = j * DMA_UNROLL + u
            _row_copy(src_hbm, row_of(r), dst_ref, r, sem).start(priority=u % 2)
        return carry
    lax.fori_loop(0, n_rows // DMA_UNROLL, trip, 0)


def _wait_row_gather(src_hbm, dst_ref, sem, n_rows):
    pltpu.make_async_copy(src_hbm.at[pl.ds(0, n_rows), :], dst_ref, sem).wait()


def _experts_kernel(tok_ref, bexp_ref, nused_ref, x_hbm, wg_ref, wu_ref, wd_ref, o_ref, xbuf_ref, sem_ref):
    i = pl.program_id(0)
    n_used = nused_ref[0]
    slot = i % 2

    def start_block(block, buf):
        base = block * MOE_ROWS
        _start_row_gather(x_hbm, lambda r: tok_ref[base + r], xbuf_ref.at[buf], sem_ref.at[buf], MOE_ROWS)

    @pl.when(i == 0)
    def _():
        start_block(0, 0)

    @pl.when(i + 1 < n_used)
    def _():
        start_block(i + 1, 1 - slot)

    @pl.when(i < n_used)
    def _():
        _wait_row_gather(x_hbm, xbuf_ref.at[slot], sem_ref.at[slot], MOE_ROWS)
        x = xbuf_ref[slot].astype(BF16)
        a = jnp.dot(x, wg_ref[...], preferred_element_type=F32)
        b = jnp.dot(x, wu_ref[...], preferred_element_type=F32)
        hid = (a * (1.0 / (1.0 + jnp.exp(-a)))) * b
        o_ref[...] = jnp.dot(hid.astype(BF16), wd_ref[...], preferred_element_type=F32)

    @pl.when(i >= n_used)
    def _():
        o_ref[...] = jnp.zeros(o_ref.shape, o_ref.dtype)


def grouped_experts(tok_buf, blk_exp, n_used, x, w_gate, w_up, w_down, n_blocks):
    d = x.shape[1]
    f = w_gate.shape[2]
    wspec = lambda a, b: pl.BlockSpec(
        (None, a, b), lambda i, tok, be, nu: (be[jnp.minimum(i, nu[0] - 1)], 0, 0))
    grid_spec = pltpu.PrefetchScalarGridSpec(
        num_scalar_prefetch=3,
        grid=(n_blocks,),
        in_specs=[pl.BlockSpec(memory_space=pl.ANY), wspec(d, f), wspec(d, f), wspec(f, d)],
        out_specs=pl.BlockSpec((MOE_ROWS, d), lambda i, *_: (i, 0)),
        scratch_shapes=[pltpu.VMEM((2, MOE_ROWS, d), x.dtype), pltpu.SemaphoreType.DMA((2,))],
    )
    return pl.pallas_call(
        _experts_kernel,
        grid_spec=grid_spec,
        out_shape=jax.ShapeDtypeStruct((n_blocks * MOE_ROWS, d), F32),
        compiler_params=_cparams("arbitrary"),
        name="moe_experts",
    )(tok_buf, blk_exp, n_used, x, w_gate, w_up, w_down)


def _combine_kernel(p0_ref, p1_ref, h_ref, g_ref, y_hbm, *rest, rows, split_step):
    o_refs, (buf_ref, sem_ref) = rest[:-2], rest[-2:]
    i = pl.program_id(0)
    base = i * rows
    _start_row_gather(y_hbm, lambda r: p0_ref[base + r], buf_ref.at[0], sem_ref.at[0], rows)
    _start_row_gather(y_hbm, lambda r: p1_ref[base + r], buf_ref.at[1], sem_ref.at[1], rows)
    g = g_ref[...]
    g0, g1 = g[:, 0:1], g[:, 1:2]
    _wait_row_gather(y_hbm, buf_ref.at[0], sem_ref.at[0], rows)
    _wait_row_gather(y_hbm, buf_ref.at[1], sem_ref.at[1], rows)
    out = h_ref[...] + (g0 * buf_ref[0] + g1 * buf_ref[1])
    if split_step is None:
        o_refs[0][...] = out
    else:
        @pl.when(i < split_step)
        def _():
            o_refs[0][...] = out

        @pl.when(i >= split_step)
        def _():
            o_refs[1][...] = out


def combine(pos0, pos1, h, gates, y_sorted, split_rows=None):
    t, d = h.shape
    rows = _pick_tile(t if split_rows is None else math.gcd(split_rows, t - split_rows), 128, 8)
    row = lambda w: pl.BlockSpec((rows, w), lambda i, *_: (i, 0))
    if split_rows is None:
        split_step = None
        out_specs = row(d)
        out_shape = jax.ShapeDtypeStruct((t, d), F32)
    else:
        split_step = split_rows // rows
        out_specs = [pl.BlockSpec((rows, d), lambda i, *_: (jnp.minimum(i, split_step - 1), 0)),
                     pl.BlockSpec((rows, d), lambda i, *_: (jnp.maximum(i - split_step, 0), 0))]
        out_shape = [jax.ShapeDtypeStruct((split_rows, d), F32), jax.ShapeDtypeStruct((t - split_rows, d), F32)]
    grid_spec = pltpu.PrefetchScalarGridSpec(
        num_scalar_prefetch=2,
        grid=(t // rows,),
        in_specs=[row(d), row(LANES), pl.BlockSpec(memory_space=pl.ANY)],
        out_specs=out_specs,
        scratch_shapes=[pltpu.VMEM((2, rows, d), F32), pltpu.SemaphoreType.DMA((2,))],
    )
    return pl.pallas_call(
        functools.partial(_combine_kernel, rows=rows, split_step=split_step),
        grid_spec=grid_spec,
        out_shape=out_shape,
        compiler_params=_cparams("arbitrary"),
        name="moe_combine",
    )(pos0, pos1, h, gates, y_sorted)


def _dispatch_plan(expert, n_tokens):
    a = expert.shape[0]
    order = jnp.argsort(expert, stable=True).astype(I32)
    rank = jnp.argsort(order).astype(I32)
    e_s = expert[order]
    experts = jnp.arange(N_EXPERTS, dtype=I32)
    start = jnp.searchsorted(e_s, experts, side='left').astype(I32)
    counts = jnp.searchsorted(e_s, experts, side='right').astype(I32) - start
    pcounts = (counts + MOE_ROWS - 1) // MOE_ROWS * MOE_ROWS
    pend = jnp.cumsum(pcounts)
    pstart = pend - pcounts
    slot = pstart[expert] + (rank - start[expert])
    n_blocks = (a + MOE_ROWS - 1) // MOE_ROWS + N_EXPERTS
    blk_exp = jnp.minimum(jnp.searchsorted(pend, jnp.arange(n_blocks, dtype=I32) * MOE_ROWS, side='right'),
                          N_EXPERTS - 1).astype(I32)
    slot_exp = jnp.repeat(blk_exp, MOE_ROWS)
    within = jnp.arange(n_blocks * MOE_ROWS, dtype=I32) - pstart[slot_exp]
    src = order[jnp.clip(start[slot_exp] + within, 0, a - 1)]
    tok_buf = jnp.where((within >= 0) & (within < counts[slot_exp]), src // 2, 0).astype(I32)
    n_used = (pend[-1] // MOE_ROWS).astype(I32).reshape(1)
    return tok_buf, slot.astype(I32), blk_exp, n_used, n_blocks


def moe_layer(h, norm_g, w_group, b_group, w_expert, b_expert, w_gate, w_up, w_down, split_rows=None):
    t, d = h.shape
    xn_f, xn_b = rmsnorm(h, norm_g, (F32, BF16))
    n_logit = N_GROUPS + N_EXPERTS
    w_r = jnp.concatenate([w_group, w_expert, jnp.zeros((d, LANES - n_logit), F32)], axis=1).astype(BF16)
    b_r = jnp.concatenate([b_group.astype(F32), b_expert.astype(F32),
                           jnp.zeros((LANES - n_logit,), F32)]).reshape(1, LANES)
    ids, gates = router(matmul(xn_b, w_r, name="router_logits"), b_r)
    tok_buf, slot, blk_exp, n_used, n_blocks = _dispatch_plan(ids[:, :2].reshape(-1), t)
    y_sorted = grouped_experts(tok_buf, blk_exp, n_used, xn_f,
                               w_gate.astype(BF16), w_up.astype(BF16), w_down.astype(BF16), n_blocks)
    slot2 = slot.reshape(t, 2)
    return combine(slot2[:, 0], slot2[:, 1], h, gates, y_sorted, split_rows)


def _query_blocks(x, n_blocks, width):
    return x.reshape(n_blocks, QBLOCK, IDX_HEADS, width).transpose(0, 2, 1, 3).reshape(
        n_blocks, IDX_HEADS * QBLOCK, width)


def kernel(x_prompt, x_sample, cache_a_k, cache_a_v, cache_b_k, cache_b_v, cache_b_kidx, attn_norm, ffn_norm, a_w_qkv, a_w_o, a_q_norm, a_k_norm, a_sinks, b_w_qkv, b_w_o, b_q_norm, b_k_norm, b_kidx_norm, moe_w_group, moe_b_group, moe_w_expert, moe_b_expert, moe_w_gate, moe_w_up, moe_w_down):
    nb, seq, d = x_prompt.shape
    db, ds, _ = x_sample.shape
    depth = attn_norm.shape[0]
    n_heads = d // HEAD_DIM
    qc = n_heads * HEAD_DIM
    kvc = N_KV * HEAD_DIM
    tp, ts = nb * seq, db * ds
    past = cache_b_k.shape[2]
    nqb = seq // QBLOCK

    h = jnp.concatenate([x_prompt.reshape(tp, d), x_sample.reshape(ts, d)], axis=0)
    pos = jnp.concatenate([jnp.tile(jnp.arange(seq), nb), jnp.tile(past + jnp.arange(ds), db)])
    head_tables = rope_tables(pos, ROT_DIM, HEAD_DIM)
    idx_tables = rope_tables(pos, IDX_ROT, IDX_DIM)

    outs_a = {"kp": [], "vp": [], "ks": [], "vs": []}
    outs_b = {"kp": [], "vp": [], "ip": [], "ks": [], "vs": [], "is": []}
    for i in range(depth):
        j = i // 2
        xn = rmsnorm(h, attn_norm[i], (BF16,))[0]
        if i % 2 == 0:
            proj = matmul(xn, a_w_qkv[j].astype(BF16), name="qkv_proj")
            q_b, k_f, k_b, v_f, v_b = qkv_post(proj, a_q_norm[j], a_k_norm[j], head_tables, n_heads,
                                               HEAD_DIM ** -0.5)
            o_p = swa_prompt(q_b, k_b, v_b, a_sinks[j], nb, seq, n_heads)
            k_all = jnp.concatenate([cache_a_k[j].astype(F32), k_f[tp:].reshape(db, ds, N_KV, HEAD_DIM)], axis=1)
            v_all = jnp.concatenate([cache_a_v[j].astype(F32), v_f[tp:].reshape(db, ds, N_KV, HEAD_DIM)], axis=1)
            o_s = swa_sample(q_b, k_all.reshape(db, WINDOW + ds, kvc).astype(BF16),
                             v_all.reshape(db, WINDOW + ds, kvc).astype(BF16), a_sinks[j], tp, db, ds, n_heads)
            outs_a["kp"].append(k_f[:tp].reshape(nb, seq, N_KV, HEAD_DIM)[:, -WINDOW:])
            outs_a["vp"].append(v_f[:tp].reshape(nb, seq, N_KV, HEAD_DIM)[:, -WINDOW:])
            outs_a["ks"].append(k_all[:, -WINDOW:])
            outs_a["vs"].append(v_all[:, -WINDOW:])
            w_o = a_w_o[j]
        else:
            n_qkv = qc + 2 * kvc
            n_rest = b_w_qkv.shape[2] - n_qkv
            rest_w = IDX_HEADS * IDX_DIM + LANES
            proj = matmul(xn, b_w_qkv[j][:, :n_qkv].astype(BF16), name="qkv_proj")
            w_rest = jnp.pad(b_w_qkv[j][:, n_qkv:], ((0, 0), (0, rest_w - n_rest))).astype(BF16)
            rest = matmul(xn, w_rest, tn_target=384, name="idx_proj")
            q_b, k_f, k_b, v_f, v_b = qkv_post(proj, b_q_norm[j], b_k_norm[j], head_tables, n_heads,
                                               HEAD_DIM ** -0.5 * LOG2E)
            qi_b, tail, kidx_b = idx_post(rest, b_kidx_norm[j], idx_tables)
            kidx_f = tail[:, :IDX_DIM]
            w_idx = tail[:, IDX_DIM:IDX_DIM + IDX_HEADS]

            blk_q = jnp.arange(nqb, dtype=I32)
            nkt_p = jnp.tile(((blk_q + 1) * QBLOCK + KEY_TILE - 1) // KEY_TILE, nb).astype(I32)
            nat_p = jnp.tile(((blk_q + 1) * QBLOCK + ATT_TILE - 1) // ATT_TILE, nb).astype(I32)
            qrow = blk_q[:, None] * QBLOCK + jnp.arange(QBLOCK, dtype=I32)[None, :]
            lim_p = jnp.tile(((qrow // CHUNK + 1) * CHUNK)[:, None, :], (nb, 1, 1)).astype(I32)
            qi_p = _query_blocks(qi_b[:tp], nb * nqb, IDX_DIM)
            w_p = _query_blocks(w_idx[:tp], nb * nqb, 1).reshape(nb * nqb, 1, IDX_HEADS * QBLOCK)
            bias_p = dsa_select(nkt_p, qi_p, w_p, lim_p, kidx_b, nb, seq, min(TOPK_MAX, seq // 4))
            o_p = dsa_attend(nat_p, q_b, k_b, v_b, bias_p, nb, n_heads)

            n_keys = past + ds
            lk = (n_keys + 2 * ATT_TILE - 1) // (2 * ATT_TILE) * (2 * ATT_TILE)
            pad_k = lambda new, cache: jnp.pad(
                jnp.concatenate([cache.reshape(db, past, -1).astype(F32), new.reshape(db, ds, -1)], axis=1),
                ((0, 0), (0, lk - n_keys), (0, 0))).astype(BF16).reshape(db * lk, -1)
            pad_q = lambda x: jnp.pad(x.reshape(db, ds, -1), ((0, 0), (0, QBLOCK - ds), (0, 0))).reshape(
                db * QBLOCK, -1)
            nkt_s = jnp.full((db,), (n_keys + KEY_TILE - 1) // KEY_TILE, I32)
            nat_s = jnp.full((db,), (n_keys + ATT_TILE - 1) // ATT_TILE, I32)
            lim_s = jnp.tile(jnp.where(jnp.arange(QBLOCK) < ds, n_keys, 0).astype(I32)[None, None, :], (db, 1, 1))
            qi_s = _query_blocks(pad_q(qi_b[tp:]), db, IDX_DIM)
            w_s = _query_blocks(pad_q(w_idx[tp:]), db, 1).reshape(db, 1, IDX_HEADS * QBLOCK)
            bias_s = dsa_select(nkt_s, qi_s, w_s, lim_s, pad_k(kidx_f[tp:], cache_b_kidx[j]), db, lk,
                                min(TOPK_MAX, n_keys // 4))
            o_s = dsa_attend(nat_s, pad_q(q_b[tp:]), pad_k(k_f[tp:], cache_b_k[j]), pad_k(v_f[tp:], cache_b_v[j]),
                             bias_s, db, n_heads)
            o_s = o_s.reshape(db, QBLOCK, qc)[:, :ds].reshape(ts, qc)

            outs_b["kp"].append(k_f[:tp].reshape(nb, seq, N_KV, HEAD_DIM))
            outs_b["vp"].append(v_f[:tp].reshape(nb, seq, N_KV, HEAD_DIM))
            outs_b["ip"].append(kidx_f[:tp].reshape(nb, seq, IDX_DIM))
            outs_b["ks"].append(k_f[tp:].reshape(db, ds, N_KV, HEAD_DIM))
            outs_b["vs"].append(v_f[tp:].reshape(db, ds, N_KV, HEAD_DIM))
            outs_b["is"].append(kidx_f[tp:].reshape(db, ds, IDX_DIM))
            w_o = b_w_o[j]
        h = matmul(jnp.concatenate([o_p, o_s], axis=0), w_o.astype(BF16), res=h, name="out_proj")
        h = moe_layer(h, ffn_norm[i], moe_w_group[i], moe_b_group[i], moe_w_expert[i], moe_b_expert[i],
                      moe_w_gate[i], moe_w_up[i], moe_w_down[i], split_rows=tp if i == depth - 1 else None)

    return (h[0].reshape(nb, seq, d), h[1].reshape(db, ds, d),
            jnp.stack(outs_a["kp"]), jnp.stack(outs_a["vp"]),
            jnp.stack(outs_b["kp"]), jnp.stack(outs_b["vp"]), jnp.stack(outs_b["ip"]),
            jnp.stack(outs_a["ks"]), jnp.stack(outs_a["vs"]),
            jnp.stack(outs_b["ks"]), jnp.stack(outs_b["vs"]), jnp.stack(outs_b["is"]))
```

```python
import functools
import math

import jax
import jax.numpy as jnp
from jax import lax
from jax.experimental import pallas as pl
from jax.experimental.pallas import tpu as pltpu

F32 = jnp.float32
BF16 = jnp.bfloat16
I32 = jnp.int32
I16 = jnp.int16

LANES = 128
CHUNK = 64
HEAD_DIM = 128
N_KV = 8
WINDOW = 128
ROT_DIM = HEAD_DIM // 4
ROPE_THETA = 500000.0
IDX_HEADS = 16
IDX_DIM = 64
IDX_ROT = IDX_DIM // 4
TOPK_MAX = 256
QBLOCK = 128
N_GROUPS = 8
EXP_PER_GROUP = 8
N_EXPERTS = N_GROUPS * EXP_PER_GROUP
RMS_EPS = 1e-6
NEG = -1e30
INT_MIN = -2 ** 31
KEY_TILE = 256
COUNT_ROWS = 1024
ATT_TILE = 512
LONG_TRIP = 4
LOG2E = 1.4426950408889634
MOE_ROWS = 256
VMEM_LIMIT = 56 * 1024 * 1024

_NT = (((1,), (1,)), ((), ()))


def _cparams(*sem):
    return pltpu.CompilerParams(dimension_semantics=sem, vmem_limit_bytes=VMEM_LIMIT)


def _pick_tile(n, target, mult):
    best = None
    t = mult
    while t <= min(n, target):
        if n % t == 0:
            best = t
        t += mult
    return best if best is not None else n


def _rmsnorm_kernel(x_ref, g_ref, *o_refs):
    x = x_ref[...]
    y = x * lax.rsqrt(jnp.mean(x * x, axis=-1, keepdims=True) + RMS_EPS) * g_ref[...]
    for o_ref in o_refs:
        o_ref[...] = y.astype(o_ref.dtype)


def rmsnorm(x, g, out_dtypes):
    t, d = x.shape
    tr = _pick_tile(t, 256, 16)
    outs = pl.pallas_call(
        _rmsnorm_kernel,
        grid=(t // tr,),
        in_specs=[pl.BlockSpec((tr, d), lambda i: (i, 0)),
                  pl.BlockSpec((1, d), lambda i: (0, 0))],
        out_specs=[pl.BlockSpec((tr, d), lambda i: (i, 0)) for _ in out_dtypes],
        out_shape=[jax.ShapeDtypeStruct((t, d), dt) for dt in out_dtypes],
        compiler_params=_cparams("parallel"),
        name="rmsnorm",
    )(x, g.reshape(1, d).astype(F32))
    return outs


def _rmsnorm_pack_kernel(x_ref, g_ref, b_ref, p_ref):
    x = x_ref[...]
    y = (x * lax.rsqrt(jnp.mean(x * x, axis=-1, keepdims=True) + RMS_EPS) * g_ref[...]).astype(BF16)
    b_ref[...] = y
    half = y.shape[1] // 2
    hi = pltpu.bitcast(y[:, :half].astype(F32), I32)
    lo = pltpu.bitcast(y[:, half:].astype(F32), I32)
    p_ref[...] = hi | lax.shift_right_logical(lo, 16)


def rmsnorm_pack(x, g):
    t, d = x.shape
    tr = _pick_tile(t, 256, 16)
    return pl.pallas_call(
        _rmsnorm_pack_kernel,
        grid=(t // tr,),
        in_specs=[pl.BlockSpec((tr, d), lambda i: (i, 0)),
                  pl.BlockSpec((1, d), lambda i: (0, 0))],
        out_specs=[pl.BlockSpec((tr, d), lambda i: (i, 0)), pl.BlockSpec((tr, d // 2), lambda i: (i, 0))],
        out_shape=[jax.ShapeDtypeStruct((t, d), BF16), jax.ShapeDtypeStruct((t, d // 2), I32)],
        compiler_params=_cparams("parallel"),
        name="rmsnorm_pack",
    )(x, g.reshape(1, d).astype(F32))


def _unpack_rows(p):
    first = pltpu.bitcast(p & jnp.int32(-65536), F32).astype(BF16)
    second = pltpu.bitcast(lax.shift_left(p, 16), F32).astype(BF16)
    return first, second


def _mm_kernel(a_ref, b_ref, o_ref):
    o_ref[...] = jnp.dot(a_ref[...], b_ref[...], preferred_element_type=F32).astype(o_ref.dtype)


def _mm_res_kernel(a_ref, b_ref, r_ref, o_ref):
    o_ref[...] = r_ref[...] + jnp.dot(a_ref[...], b_ref[...], preferred_element_type=F32)


def matmul(a, b, res=None, tm_target=1280, tn_target=512, name="matmul"):
    m, k = a.shape
    _, n = b.shape
    tm = _pick_tile(m, tm_target, 16)
    tn = _pick_tile(n, tn_target, LANES)
    in_specs = [pl.BlockSpec((tm, k), lambda i, j: (i, 0)),
                pl.BlockSpec((k, tn), lambda i, j: (0, j))]
    args = [a, b]
    if res is not None:
        in_specs.append(pl.BlockSpec((tm, tn), lambda i, j: (i, j)))
        args.append(res)
    return pl.pallas_call(
        _mm_kernel if res is None else _mm_res_kernel,
        grid=(m // tm, n // tn),
        in_specs=in_specs,
        out_specs=pl.BlockSpec((tm, tn), lambda i, j: (i, j)),
        out_shape=jax.ShapeDtypeStruct((m, n), F32),
        compiler_params=_cparams("parallel", "arbitrary"),
        name=name,
    )(*args)


def rope_tables(pos, rot_dim, period):
    half = rot_dim // 2
    inv_freq = 1.0 / (ROPE_THETA ** (jnp.arange(half, dtype=F32) * (2.0 / rot_dim)))
    ang = pos.astype(F32)[:, None] * inv_freq[None, :]
    cos, sin = jnp.cos(ang), jnp.sin(ang)
    t = pos.shape[0]
    rest = period - rot_dim
    c = jnp.concatenate([cos, cos, jnp.ones((t, rest), F32)], axis=1)
    s1 = jnp.concatenate([-sin, jnp.zeros((t, half + rest), F32)], axis=1)
    s2 = jnp.concatenate([jnp.zeros((t, half), F32), sin, jnp.zeros((t, rest), F32)], axis=1)
    rep = LANES // period
    return tuple(jnp.tile(x, (1, rep)) for x in (c, s1, s2))


def _rope(y, c, s1, s2, half):
    return y * c + pltpu.roll(y, LANES - half, 1) * s1 + pltpu.roll(y, half, 1) * s2


def _qkv_post_kernel(p_ref, qg_ref, kg_ref, c_ref, s1_ref, s2_ref,
                     q_ref, kf_ref, kb_ref, vf_ref, vb_ref, *, n_heads, q_scale):
    c, s1, s2 = c_ref[...], s1_ref[...], s2_ref[...]

    def normrope(x, g):
        y = x * lax.rsqrt(jnp.mean(x * x, axis=-1, keepdims=True) + RMS_EPS) * g
        return _rope(y, c, s1, s2, ROT_DIM // 2)

    qg, kg = qg_ref[...], kg_ref[...]
    for h in range(n_heads):
        sl = slice(h * HEAD_DIM, (h + 1) * HEAD_DIM)
        q_ref[:, sl] = (normrope(p_ref[:, sl], qg) * q_scale).astype(BF16)
    for h in range(N_KV):
        sl = slice(h * HEAD_DIM, (h + 1) * HEAD_DIM)
        k = normrope(p_ref[:, (n_heads + h) * HEAD_DIM:(n_heads + h + 1) * HEAD_DIM], kg)
        kf_ref[:, h, :] = k
        kb_ref[:, sl] = k.astype(BF16)
        v = p_ref[:, (n_heads + N_KV + h) * HEAD_DIM:(n_heads + N_KV + h + 1) * HEAD_DIM]
        vf_ref[:, h, :] = v
        vb_ref[:, sl] = v.astype(BF16)


def qkv_post(proj, q_gain, k_gain, tables, n_heads, q_scale):
    t, ncol = proj.shape
    tr = _pick_tile(t, 256, 16)
    kvc = N_KV * HEAD_DIM
    qc = n_heads * HEAD_DIM
    row = lambda w: pl.BlockSpec((tr, w), lambda i: (i, 0))
    row3 = pl.BlockSpec((tr, N_KV, HEAD_DIM), lambda i: (i, 0, 0))
    vec = pl.BlockSpec((1, HEAD_DIM), lambda i: (0, 0))
    cache = jax.ShapeDtypeStruct((t, N_KV, HEAD_DIM), F32)
    return pl.pallas_call(
        functools.partial(_qkv_post_kernel, n_heads=n_heads, q_scale=q_scale),
        grid=(t // tr,),
        in_specs=[row(ncol), vec, vec, row(LANES), row(LANES), row(LANES)],
        out_specs=[row(qc), row3, row(kvc), row3, row(kvc)],
        out_shape=[jax.ShapeDtypeStruct((t, qc), BF16),
                   cache, jax.ShapeDtypeStruct((t, kvc), BF16),
                   cache, jax.ShapeDtypeStruct((t, kvc), BF16)],
        compiler_params=_cparams("parallel"),
        name="qkv_post",
    )(proj, q_gain.reshape(1, HEAD_DIM).astype(F32), k_gain.reshape(1, HEAD_DIM).astype(F32), *tables)


def _cache_rows_kernel(x_ref, o_ref):
    for h in range(N_KV):
        o_ref[:, h * HEAD_DIM:(h + 1) * HEAD_DIM] = x_ref[:, h, :].astype(BF16)


def cache_rows(c):
    r = c.shape[0]
    tr = _pick_tile(r, 256, 16)
    return pl.pallas_call(
        _cache_rows_kernel,
        grid=(r // tr,),
        in_specs=[pl.BlockSpec((tr, N_KV, HEAD_DIM), lambda i: (i, 0, 0))],
        out_specs=pl.BlockSpec((tr, N_KV * HEAD_DIM), lambda i: (i, 0)),
        out_shape=jax.ShapeDtypeStruct((r, N_KV * HEAD_DIM), BF16),
        compiler_params=_cparams("parallel"),
        name="cache_rows",
    )(c)


IDX_W_SCALE = (IDX_HEADS ** -0.5) * (IDX_DIM ** -0.5)


def _idx_post_kernel(r_ref, g_ref, c_ref, s1_ref, s2_ref, qi_ref, tail_ref, kb_ref):
    c, s1, s2 = c_ref[...], s1_ref[...], s2_ref[...]
    nq = IDX_HEADS * IDX_DIM
    for j in range(nq // LANES):
        sl = slice(j * LANES, (j + 1) * LANES)
        qi_ref[:, sl] = _rope(r_ref[:, sl], c, s1, s2, IDX_ROT // 2).astype(BF16)
    t = r_ref[:, nq:nq + LANES]
    lane = lax.broadcasted_iota(I32, t.shape, 1)
    is_k = lane < IDX_DIM
    ms = jnp.sum(jnp.where(is_k, t * t, 0.0), axis=-1, keepdims=True) * (1.0 / IDX_DIM)
    kn = t * lax.rsqrt(ms + RMS_EPS) * g_ref[...]
    tail = jnp.where(is_k, _rope(kn, c, s1, s2, IDX_ROT // 2), t * IDX_W_SCALE)
    tail_ref[...] = tail
    kb_ref[...] = tail[:, :IDX_DIM].astype(BF16)


def idx_post(rest, kidx_gain, tables):
    t, ncol = rest.shape
    tr = _pick_tile(t, 256, 16)
    nq = IDX_HEADS * IDX_DIM
    row = lambda w: pl.BlockSpec((tr, w), lambda i: (i, 0))
    g = jnp.concatenate([kidx_gain.astype(F32), jnp.zeros((LANES - IDX_DIM,), F32)]).reshape(1, LANES)
    return pl.pallas_call(
        _idx_post_kernel,
        grid=(t // tr,),
        in_specs=[row(ncol), pl.BlockSpec((1, LANES), lambda i: (0, 0)), row(LANES), row(LANES), row(LANES)],
        out_specs=[row(nq), row(LANES), row(IDX_DIM)],
        out_shape=[jax.ShapeDtypeStruct((t, nq), BF16), jax.ShapeDtypeStruct((t, LANES), F32),
                   jax.ShapeDtypeStruct((t, IDX_DIM), BF16)],
        compiler_params=_cparams("parallel"),
        name="idx_post",
    )(rest, g, *tables)


def _sink_attn(q, k, v, bias, sink):
    s = lax.dot_general(q, k, _NT, preferred_element_type=F32)
    if bias is not None:
        s = s + bias
    m = jnp.maximum(jnp.max(s, axis=-1, keepdims=True), sink)
    p = jnp.exp(s - m)
    denom = jnp.sum(p, axis=-1, keepdims=True) + jnp.exp(sink - m)
    o = jnp.dot(p.astype(BF16), v, preferred_element_type=F32)
    return o / denom


def _swa_prompt_kernel(sink_ref, q_ref, kp_ref, kc_ref, vp_ref, vc_ref, o_ref, *, group):
    i = pl.program_id(1)
    rows, cols = QBLOCK, 2 * QBLOCK
    qc = lax.broadcasted_iota(I32, (rows, cols), 0) // CHUNK
    kc = lax.broadcasted_iota(I32, (rows, cols), 1) // CHUNK
    first_valid = jnp.where(i > 0, 0, QBLOCK // CHUNK)
    vis = (kc >= qc) & (kc <= qc + WINDOW // CHUNK) & (kc >= first_valid)
    bias = jnp.where(vis, 0.0, NEG)
    for kv in range(N_KV):
        sl = slice(kv * HEAD_DIM, (kv + 1) * HEAD_DIM)
        k = jnp.concatenate([kp_ref[:, sl], kc_ref[:, sl]], axis=0)
        v = jnp.concatenate([vp_ref[:, sl], vc_ref[:, sl]], axis=0)
        for g in range(group):
            h = kv * group + g
            hs = slice(h * HEAD_DIM, (h + 1) * HEAD_DIM)
            o_ref[:, hs] = _sink_attn(q_ref[:, hs], k, v, bias, sink_ref[h]).astype(BF16)


def swa_prompt(q, k, v, sinks, n_batch, seq, n_heads):
    nqb = seq // QBLOCK
    qc = n_heads * HEAD_DIM
    kvc = N_KV * HEAD_DIM
    cur = lambda w: pl.BlockSpec((QBLOCK, w), lambda b, i: (b * nqb + i, 0))
    prev = lambda w: pl.BlockSpec((QBLOCK, w), lambda b, i: (b * nqb + jnp.maximum(i - 1, 0), 0))
    return pl.pallas_call(
        functools.partial(_swa_prompt_kernel, group=n_heads // N_KV),
        grid=(n_batch, nqb),
        in_specs=[pl.BlockSpec(memory_space=pltpu.SMEM),
                  cur(qc), prev(kvc), cur(kvc), prev(kvc), cur(kvc)],
        out_specs=cur(qc),
        out_shape=jax.ShapeDtypeStruct((n_batch * seq, qc), BF16),
        compiler_params=_cparams("parallel", "arbitrary"),
        name="swa_prompt",
    )(sinks.astype(F32), q, k, k, v, v)


def _swa_sample_kernel(sink_ref, q_ref, k_ref, v_ref, o_ref, *, group):
    for kv in range(N_KV):
        sl = slice(kv * HEAD_DIM, (kv + 1) * HEAD_DIM)
        k, v = k_ref[:, sl], v_ref[:, sl]
        for g in range(group):
            h = kv * group + g
            hs = slice(h * HEAD_DIM, (h + 1) * HEAD_DIM)
            o_ref[:, hs] = _sink_attn(q_ref[:, hs], k, v, None, sink_ref[h]).astype(BF16)


def swa_sample(q, k_all, v_all, sinks, row0, n_batch, n_new, n_heads):
    qc = n_heads * HEAD_DIM
    nk = k_all.shape[1]
    kvc = N_KV * HEAD_DIM
    return pl.pallas_call(
        functools.partial(_swa_sample_kernel, group=n_heads // N_KV),
        grid=(n_batch,),
        in_specs=[pl.BlockSpec(memory_space=pltpu.SMEM),
                  pl.BlockSpec((n_new, qc), lambda b: (row0 // n_new + b, 0)),
                  pl.BlockSpec((None, nk, kvc), lambda b: (b, 0, 0)),
                  pl.BlockSpec((None, nk, kvc), lambda b: (b, 0, 0))],
        out_specs=pl.BlockSpec((n_new, qc), lambda b: (b, 0)),
        out_shape=jax.ShapeDtypeStruct((n_batch * n_new, qc), BF16),
        compiler_params=_cparams("parallel"),
        name="swa_sample",
    )(sinks.astype(F32), q, k_all, v_all)


def _select_kernel(nkt_ref, qi_ref, w_ref, lim_ref, kidx_ref, bias_ref, keys_ref, hi_ref, lo_ref, p_ref,
                   *, topk, n_tiles, idx_bits):
    tk = KEY_TILE
    blk = pl.program_id(0) * pl.num_programs(1) + pl.program_id(1)
    nkt = nkt_ref[blk]
    qi = qi_ref[...]
    w = w_ref[...]
    lim = lim_ref[...]
    imin = jnp.int32(INT_MIN)
    half_min = -2 ** 15

    def tile_rows(kt):
        return pl.ds(pl.multiple_of(kt * tk, tk), tk)

    def key_index(kt):
        return kt * tk + lax.broadcasted_iota(I32, (tk, LANES), 0)

    def score_tile(kt, carry):
        d = lax.dot_general(kidx_ref[tile_rows(kt), :], qi, _NT, preferred_element_type=F32)
        r = jnp.maximum(d, 0.0) * w
        acc = r[:, 0:LANES]
        for h in range(1, IDX_HEADS):
            acc = acc + r[:, h * LANES:(h + 1) * LANES]
        bits = pltpu.bitcast(acc, I32)
        key = bits ^ ((bits >> 31) & jnp.int32(0x7FFFFFFF))
        key = jnp.where(key_index(kt) < lim, key, imin)
        keys_ref[tile_rows(kt), :] = key
        hi_ref[tile_rows(kt), :] = (key >> 16).astype(I16)
        lo_ref[tile_rows(kt), :] = ((key & 0xFFFF) + half_min).astype(I16)
        return carry

    lax.fori_loop(0, nkt, score_tile, 0)

    def count(pred):
        def body(kt, acc):
            c = jnp.where(pred(keys_ref[tile_rows(kt), :], key_index(kt)), 1, 0).astype(I32)
            return acc + jnp.sum(c.reshape(tk // 8, 8, LANES), axis=0)
        acc = lax.fori_loop(0, nkt, body, jnp.zeros((8, LANES), I32))
        return jnp.sum(acc, axis=0, keepdims=True)

    per_trip = COUNT_ROWS // tk
    n_trips = (nkt + per_trip - 1) // per_trip

    def trip_rows(j):
        return pl.ds(pl.multiple_of(j * COUNT_ROWS, COUNT_ROWS), COUNT_ROWS)

    def neutral_tile(kt, carry):
        hi_ref[tile_rows(kt), :] = jnp.full((tk, LANES), half_min, I16)
        lo_ref[tile_rows(kt), :] = jnp.full((tk, LANES), half_min, I16)
        return carry

    lax.fori_loop(nkt, n_trips * per_trip, neutral_tile, 0)

    def count_ge16(ref, cand):
        c16 = cand.astype(I16)
        pack = 16

        def body(j, acc):
            c = jnp.where(ref[trip_rows(j), :] >= c16, jnp.int16(1), jnp.int16(0))
            parts = [c[r * pack:(r + 1) * pack, :] for r in range(COUNT_ROWS // pack)]
            while len(parts) > 1:
                parts = [parts[a] + parts[a + 1] for a in range(0, len(parts), 2)]
            return acc + parts[0]
        acc = lax.fori_loop(0, n_trips, body, jnp.zeros((pack, LANES), I16))
        return jnp.sum(acc.astype(I32), axis=0, keepdims=True)

    def search16(ref, want):
        def bit(i, v):
            cand = v + lax.shift_left(jnp.int32(1), 15 - i)
            return jnp.where(count_ge16(ref, cand) >= want, cand, v)
        return lax.fori_loop(0, 16, bit, jnp.full((1, LANES), half_min, I32))

    tau_hi = search16(hi_ref, topk)
    n_above = count_ge16(hi_ref, tau_hi + 1)
    n_above = jnp.where(tau_hi == 2 ** 15 - 1, 0, n_above)
    tau_hi16 = tau_hi.astype(I16)

    def mask_low(kt, carry):
        rows = tile_rows(kt)
        lo_ref[rows, :] = jnp.where(hi_ref[rows, :] == tau_hi16, lo_ref[rows, :], jnp.int16(half_min))
        return carry

    lax.fori_loop(0, nkt, mask_low, 0)
    tau_lo = search16(lo_ref, topk - n_above)
    tau = lax.shift_left(tau_hi, 16) | (tau_lo - half_min)

    n_ge = count(lambda kk, _: (kk >= tau) & (kk != imin))
    n_gt = count(lambda kk, _: kk > tau)
    need = topk - n_gt

    p_ref[...] = jnp.full((1, LANES), 2 ** 30, I32)

    @pl.when(jnp.max(n_ge) > topk)
    def _():
        def index_bit(i, p):
            cand = p + lax.shift_left(jnp.int32(1), idx_bits - 1 - i)
            cnt = count(lambda kk, si: (kk == tau) & (si < cand))
            return jnp.where(cnt < need, cand, p)
        p_ref[...] = lax.fori_loop(0, idx_bits, index_bit, jnp.zeros((1, LANES), I32))

    p_last = p_ref[...]

    def write_tile(kt, carry):
        kk = keys_ref[tile_rows(kt), :]
        sel = (kk > tau) | ((kk == tau) & (key_index(kt) <= p_last) & (kk != imin))
        bias_ref[tile_rows(kt), :] = jnp.where(sel, 0.0, NEG).astype(BF16)
        return carry

    lax.fori_loop(0, nkt, write_tile, 0)

    def fill_tile(kt, carry):
        bias_ref[tile_rows(kt), :] = jnp.full((tk, LANES), NEG, BF16)
        return carry

    lax.fori_loop(nkt, n_tiles, fill_tile, 0)


def dsa_select(nkt, qi_blocks, w_blocks, limits, kidx, nb, lk, topk):
    nq = qi_blocks.shape[0] // nb
    assert lk % COUNT_ROWS == 0, "the counting passes walk whole COUNT_ROWS-row trips"
    n_tiles = lk // KEY_TILE
    idx_bits = max(1, int(lk).bit_length())
    per_block = lambda shape: pl.BlockSpec((None,) + shape, lambda b, q, *_: (b * nq + q, 0, 0))
    grid_spec = pltpu.PrefetchScalarGridSpec(
        num_scalar_prefetch=1,
        grid=(nb, nq),
        in_specs=[per_block((IDX_HEADS * QBLOCK, IDX_DIM)),
                  per_block((1, IDX_HEADS * QBLOCK)),
                  per_block((1, LANES)),
                  pl.BlockSpec((lk, IDX_DIM), lambda b, q, *_: (b, 0))],
        out_specs=per_block((lk, LANES)),
        scratch_shapes=[pltpu.VMEM((lk, LANES), I32), pltpu.VMEM((lk, LANES), I16), pltpu.VMEM((lk, LANES), I16),
                        pltpu.VMEM((1, LANES), I32)],
    )
    return pl.pallas_call(
        functools.partial(_select_kernel, topk=topk, n_tiles=n_tiles, idx_bits=idx_bits),
        grid_spec=grid_spec,
        out_shape=jax.ShapeDtypeStruct((nb * nq, lk, LANES), BF16),
        compiler_params=_cparams("parallel", "arbitrary"),
        name="dsa_select",
    )(nkt, qi_blocks, w_blocks, limits, kidx)


def _masked_attn_kernel(nkt_ref, q_ref, k_ref, v_ref, bias_ref, o_ref, m_ref, acc_ref, s0_ref, s1_ref,
                        *, group, n_tiles):
    tk = ATT_TILE
    blk = pl.program_id(0) * pl.num_programs(2) + pl.program_id(2)
    nkt = nkt_ref[blk]
    rows = group * QBLOCK
    eye = jnp.where(lax.broadcasted_iota(I32, (QBLOCK, QBLOCK), 0)
                    == lax.broadcasted_iota(I32, (QBLOCK, QBLOCK), 1), 1.0, 0.0).astype(BF16)
    q_aug = jnp.concatenate(
        [jnp.concatenate([q_ref[:, g * HEAD_DIM:(g + 1) * HEAD_DIM], eye], axis=1) for g in range(group)],
        axis=0)
    ones = jnp.ones((tk, HEAD_DIM), BF16)
    m_ref[...] = jnp.full((rows, LANES), NEG, F32)
    acc_ref[...] = jnp.zeros((rows, 2 * HEAD_DIM), F32)

    def tile_rows(kt):
        return pl.ds(pl.multiple_of(jnp.minimum(kt, n_tiles - 1) * tk, tk), tk)

    def scores(kt):
        sl = tile_rows(kt)
        k_aug = jnp.concatenate([k_ref[sl, :], bias_ref[sl, :]], axis=1)
        return lax.dot_general(q_aug, k_aug, _NT, preferred_element_type=F32)

    def update(s_ref, kt):
        s = s_ref[...]
        m_prev = m_ref[...]
        m_new = jnp.maximum(m_prev, jnp.max(s, axis=-1, keepdims=True))
        alpha = jnp.exp2(m_prev - m_new)
        p = jnp.exp2(s - jnp.concatenate([m_new] * (tk // LANES), axis=1))
        v_aug = jnp.concatenate([v_ref[tile_rows(kt), :], ones], axis=1)
        pv = jnp.dot(p.astype(BF16), v_aug, preferred_element_type=F32)
        acc_ref[...] = jnp.concatenate([alpha, alpha], axis=1) * acc_ref[...] + pv
        m_ref[...] = m_new

    s0_ref[...] = scores(0)

    def run(first, n_trips, per_trip):
        def body(u, carry):
            base = first + per_trip * u
            for i in range(0, per_trip, 2):
                s1_ref[...] = scores(base + i + 1)
                update(s0_ref, base + i)
                s0_ref[...] = scores(base + i + 2)
                update(s1_ref, base + i + 1)
            return carry
        lax.fori_loop(0, n_trips, body, 0)

    n_long = nkt // LONG_TRIP
    run(0, n_long, LONG_TRIP)
    run(n_long * LONG_TRIP, (nkt - n_long * LONG_TRIP + 1) // 2, 2)
    acc = acc_ref[...]
    o = acc[:, :HEAD_DIM] / jnp.maximum(acc[:, HEAD_DIM:], 1e-30)
    for g in range(group):
        o_ref[:, g * HEAD_DIM:(g + 1) * HEAD_DIM] = o[g * QBLOCK:(g + 1) * QBLOCK].astype(BF16)


def dsa_attend(nkt, q, k, v, bias, nb, n_heads):
    lk = bias.shape[1]
    nq = bias.shape[0] // nb
    group = n_heads // N_KV
    gw = group * HEAD_DIM
    rows = group * QBLOCK
    n_tiles = lk // ATT_TILE
    assert lk % (2 * ATT_TILE) == 0, "the tile-pair loop needs an even number of key tiles"
    qspec = pl.BlockSpec((QBLOCK, gw), lambda b, h, q, *_: (b * nq + q, h))
    kvspec = pl.BlockSpec((lk, HEAD_DIM), lambda b, h, q, *_: (b, h))
    grid_spec = pltpu.PrefetchScalarGridSpec(
        num_scalar_prefetch=1,
        grid=(nb, N_KV, nq),
        in_specs=[qspec, kvspec, kvspec,
                  pl.BlockSpec((None, lk, LANES), lambda b, h, q, *_: (b * nq + q, 0, 0))],
        out_specs=qspec,
        scratch_shapes=[pltpu.VMEM((rows, LANES), F32),
                        pltpu.VMEM((rows, 2 * HEAD_DIM), F32),
                        pltpu.VMEM((rows, ATT_TILE), F32),
                        pltpu.VMEM((rows, ATT_TILE), F32)],
    )
    return pl.pallas_call(
        functools.partial(_masked_attn_kernel, group=group, n_tiles=n_tiles),
        grid_spec=grid_spec,
        out_shape=jax.ShapeDtypeStruct((nb * nq * QBLOCK, q.shape[1]), BF16),
        compiler_params=_cparams("parallel", "parallel", "arbitrary"),
        name="dsa_attend",
    )(nkt, q, k, v, bias)


def _router_kernel(l_ref, b_ref, id_ref, gate_ref):
    x = l_ref[...] + b_ref[...]
    lane = lax.broadcasted_iota(I32, x.shape, 1)
    big = jnp.int32(2 ** 30)
    ninf = -jnp.inf

    def first_max(vals):
        m = jnp.max(vals, axis=-1, keepdims=True)
        return m, jnp.min(jnp.where(vals == m, lane, big), axis=-1, keepdims=True)

    glog = jnp.where(lane < N_GROUPS, x, ninf)
    gmax, gsel = first_max(glog)
    gprob = 1.0 / jnp.sum(jnp.exp(glog - gmax), axis=-1, keepdims=True)
    in_group = (lane >= N_GROUPS) & ((lane - N_GROUPS) // EXP_PER_GROUP == gsel) & (lane < N_GROUPS + N_EXPERTS)
    e1 = jnp.where(in_group, x, ninf)
    v1, i1 = first_max(e1)
    v2, i2 = first_max(jnp.where(lane == i1, ninf, e1))
    t = jnp.exp(v2 - v1)
    g1 = gprob / (1.0 + t)
    g2 = gprob * t / (1.0 + t)
    id_ref[...] = jnp.where(lane == 0, i1 - N_GROUPS, jnp.where(lane == 1, i2 - N_GROUPS, 0))
    gate_ref[...] = jnp.where(lane == 0, g1, jnp.where(lane == 1, g2, 0.0))


def router(logits, bias):
    t = logits.shape[0]
    tr = _pick_tile(t, 256, 8)
    row = pl.BlockSpec((tr, LANES), lambda i: (i, 0))
    return pl.pallas_call(
        _router_kernel,
        grid=(t // tr,),
        in_specs=[row, pl.BlockSpec((1, LANES), lambda i: (0, 0))],
        out_specs=[row, row],
        out_shape=[jax.ShapeDtypeStruct((t, LANES), I32), jax.ShapeDtypeStruct((t, LANES), F32)],
        compiler_params=_cparams("parallel"),
        name="router",
    )(logits, bias)


DMA_UNROLL = 8


def _row_copy(src_hbm, row, dst_ref, r, sem):
    return pltpu.make_async_copy(src_hbm.at[pl.ds(row, 1), :], dst_ref.at[pl.ds(r, 1), :], sem)


def _start_row_gather(src_hbm, row_of, dst_ref, sem, n_rows):
    def trip(j, carry):
        for u in range(DMA_UNROLL):
            r = j * DMA_UNROLL + u
            _row_copy(src_hbm, row_of(r), dst_ref, r, sem).start(priority=u % 2)
        return carry
    lax.fori_loop(0, n_rows // DMA_UNROLL, trip, 0)


def _wait_row_gather(src_hbm, dst_ref, sem, n_rows):
    pltpu.make_async_copy(src_hbm.at[pl.ds(0, n_rows), :], dst_ref, sem).wait()


def _experts_kernel(tok_ref, bexp_ref, nused_ref, x_hbm, wg_ref, wu_ref, wd_ref, o_ref, xbuf_ref, sem_ref):
    i, j = pl.program_id(0), pl.program_id(1)
    n_used = nused_ref[0]
    slot = i % 2
    first = j == 0

    def start_block(block, buf):
        base = block * MOE_ROWS
        _start_row_gather(x_hbm, lambda r: tok_ref[base + r], xbuf_ref.at[buf], sem_ref.at[buf], MOE_ROWS)

    @pl.when(first & (i == 0))
    def _():
        start_block(0, 0)

    @pl.when(first & (i + 1 < n_used))
    def _():
        start_block(i + 1, 1 - slot)

    @pl.when(first & (i < n_used))
    def _():
        _wait_row_gather(x_hbm, xbuf_ref.at[slot], sem_ref.at[slot], MOE_ROWS)

    @pl.when(i < n_used)
    def _():
        x0, x1 = _unpack_rows(xbuf_ref[slot])
        half = x0.shape[1]

        def proj(w_ref):
            return (jnp.dot(x0, w_ref[:half, :].astype(BF16), preferred_element_type=F32)
                    + jnp.dot(x1, w_ref[half:, :].astype(BF16), preferred_element_type=F32))

        a = proj(wg_ref)
        b = proj(wu_ref)
        hid = (a * (1.0 / (1.0 + jnp.exp(-a)))) * b
        y = jnp.dot(hid.astype(BF16), wd_ref[...].astype(BF16), preferred_element_type=F32)

        @pl.when(first)
        def _():
            o_ref[...] = y

        @pl.when(jnp.logical_not(first))
        def _():
            o_ref[...] += y

    @pl.when(first & (i >= n_used))
    def _():
        o_ref[...] = jnp.zeros(o_ref.shape, o_ref.dtype)


def grouped_experts(tok_buf, blk_exp, n_used, x, w_gate, w_up, w_down, n_blocks):
    d, f = w_gate.shape[1], w_gate.shape[2]
    fh = f // 2

    def expert_of(i, be, nu):
        return be[jnp.minimum(i, nu[0] - 1)]

    def half_of(i, j, nu):
        ii = jnp.minimum(i, nu[0] - 1)
        jj = jnp.where(i < nu[0], j, 1)
        return jnp.where(ii % 2 == 0, jj, 1 - jj)

    w_in = pl.BlockSpec((None, d, fh), lambda i, j, tok, be, nu: (expert_of(i, be, nu), 0, half_of(i, j, nu)))
    w_out = pl.BlockSpec((None, fh, d), lambda i, j, tok, be, nu: (expert_of(i, be, nu), half_of(i, j, nu), 0))
    grid_spec = pltpu.PrefetchScalarGridSpec(
        num_scalar_prefetch=3,
        grid=(n_blocks, 2),
        in_specs=[pl.BlockSpec(memory_space=pl.ANY), w_in, w_in, w_out],
        out_specs=pl.BlockSpec((MOE_ROWS, d), lambda i, j, *_: (i, 0)),
        scratch_shapes=[pltpu.VMEM((2, MOE_ROWS, x.shape[1]), x.dtype), pltpu.SemaphoreType.DMA((2,))],
    )
    return pl.pallas_call(
        _experts_kernel,
        grid_spec=grid_spec,
        out_shape=jax.ShapeDtypeStruct((n_blocks * MOE_ROWS, d), F32),
        compiler_params=_cparams("arbitrary", "arbitrary"),
        name="moe_experts",
    )(tok_buf, blk_exp, n_used, x, w_gate, w_up, w_down)


def _combine_kernel(p0_ref, p1_ref, h_ref, g_ref, y_hbm, *rest, rows, split_step):
    o_refs, (buf_ref, sem_ref) = rest[:-2], rest[-2:]
    i = pl.program_id(0)
    base = i * rows
    _start_row_gather(y_hbm, lambda r: p0_ref[base + r], buf_ref.at[0], sem_ref.at[0], rows)
    _start_row_gather(y_hbm, lambda r: p1_ref[base + r], buf_ref.at[1], sem_ref.at[1], rows)
    g = g_ref[...]
    g0, g1 = g[:, 0:1], g[:, 1:2]
    _wait_row_gather(y_hbm, buf_ref.at[0], sem_ref.at[0], rows)
    _wait_row_gather(y_hbm, buf_ref.at[1], sem_ref.at[1], rows)
    out = h_ref[...] + (g0 * buf_ref[0] + g1 * buf_ref[1])
    if split_step is None:
        o_refs[0][...] = out
    else:
        @pl.when(i < split_step)
        def _():
            o_refs[0][...] = out

        @pl.when(i >= split_step)
        def _():
            o_refs[1][...] = out


def combine(pos0, pos1, h, gates, y_sorted, split_rows=None):
    t, d = h.shape
    rows = _pick_tile(t if split_rows is None else math.gcd(split_rows, t - split_rows), 128, 8)
    row = lambda w: pl.BlockSpec((rows, w), lambda i, *_: (i, 0))
    if split_rows is None:
        split_step = None
        out_specs = row(d)
        out_shape = jax.ShapeDtypeStruct((t, d), F32)
    else:
        split_step = split_rows // rows
        out_specs = [pl.BlockSpec((rows, d), lambda i, *_: (jnp.minimum(i, split_step - 1), 0)),
                     pl.BlockSpec((rows, d), lambda i, *_: (jnp.maximum(i - split_step, 0), 0))]
        out_shape = [jax.ShapeDtypeStruct((split_rows, d), F32), jax.ShapeDtypeStruct((t - split_rows, d), F32)]
    grid_spec = pltpu.PrefetchScalarGridSpec(
        num_scalar_prefetch=2,
        grid=(t // rows,),
        in_specs=[row(d), row(LANES), pl.BlockSpec(memory_space=pl.ANY)],
        out_specs=out_specs,
        scratch_shapes=[pltpu.VMEM((2, rows, d), F32), pltpu.SemaphoreType.DMA((2,))],
    )
    return pl.pallas_call(
        functools.partial(_combine_kernel, rows=rows, split_step=split_step),
        grid_spec=grid_spec,
        out_shape=out_shape,
        compiler_params=_cparams("arbitrary"),
        name="moe_combine",
    )(pos0, pos1, h, gates, y_sorted)


def _dispatch_plan(expert, n_tokens):
    a = expert.shape[0]
    order = jnp.argsort(expert, stable=True).astype(I32)
    rank = jnp.argsort(order).astype(I32)
    e_s = expert[order]
    experts = jnp.arange(N_EXPERTS, dtype=I32)
    start = jnp.searchsorted(e_s, experts, side='left').astype(I32)
    counts = jnp.searchsorted(e_s, experts, side='right').astype(I32) - start
    pcounts = (counts + MOE_ROWS - 1) // MOE_ROWS * MOE_ROWS
    pend = jnp.cumsum(pcounts)
    pstart = pend - pcounts
    slot = pstart[expert] + (rank - start[expert])
    n_blocks = (a + MOE_ROWS - 1) // MOE_ROWS + N_EXPERTS
    blk_exp = jnp.minimum(jnp.searchsorted(pend, jnp.arange(n_blocks, dtype=I32) * MOE_ROWS, side='right'),
                          N_EXPERTS - 1).astype(I32)
    slot_exp = jnp.repeat(blk_exp, MOE_ROWS)
    within = jnp.arange(n_blocks * MOE_ROWS, dtype=I32) - pstart[slot_exp]
    src = order[jnp.clip(start[slot_exp] + within, 0, a - 1)]
    tok_buf = jnp.where((within >= 0) & (within < counts[slot_exp]), src // 2, 0).astype(I32)
    n_used = (pend[-1] // MOE_ROWS).astype(I32).reshape(1)
    return tok_buf, slot.astype(I32), blk_exp, n_used, n_blocks


def moe_layer(h, norm_g, w_group, b_group, w_expert, b_expert, w_gate, w_up, w_down, split_rows=None):
    t, d = h.shape
    xn_b, xn_p = rmsnorm_pack(h, norm_g)
    n_logit = N_GROUPS + N_EXPERTS
    w_r = jnp.concatenate([w_group, w_expert, jnp.zeros((d, LANES - n_logit), F32)], axis=1).astype(BF16)
    b_r = jnp.concatenate([b_group.astype(F32), b_expert.astype(F32),
                           jnp.zeros((LANES - n_logit,), F32)]).reshape(1, LANES)
    ids, gates = router(matmul(xn_b, w_r, name="router_logits"), b_r)
    tok_buf, slot, blk_exp, n_used, n_blocks = _dispatch_plan(ids[:, :2].reshape(-1), t)
    y_sorted = grouped_experts(tok_buf, blk_exp, n_used, xn_p, w_gate, w_up, w_down, n_blocks)
    slot2 = slot.reshape(t, 2)
    return combine(slot2[:, 0], slot2[:, 1], h, gates, y_sorted, split_rows)


def _query_blocks(x, n_blocks, width):
    return x.reshape(n_blocks, QBLOCK, IDX_HEADS, width).transpose(0, 2, 1, 3).reshape(
        n_blocks, IDX_HEADS * QBLOCK, width)


def kernel(x_prompt, x_sample, cache_a_k, cache_a_v, cache_b_k, cache_b_v, cache_b_kidx, attn_norm, ffn_norm, a_w_qkv, a_w_o, a_q_norm, a_k_norm, a_sinks, b_w_qkv, b_w_o, b_q_norm, b_k_norm, b_kidx_norm, moe_w_group, moe_b_group, moe_w_expert, moe_b_expert, moe_w_gate, moe_w_up, moe_w_down):
    nb, seq, d = x_prompt.shape
    db, ds, _ = x_sample.shape
    depth = attn_norm.shape[0]
    n_heads = d // HEAD_DIM
    qc = n_heads * HEAD_DIM
    kvc = N_KV * HEAD_DIM
    tp, ts = nb * seq, db * ds
    past = cache_b_k.shape[2]
    nqb = seq // QBLOCK

    h = jnp.concatenate([x_prompt.reshape(tp, d), x_sample.reshape(ts, d)], axis=0)
    pos = jnp.concatenate([jnp.tile(jnp.arange(seq), nb), jnp.tile(past + jnp.arange(ds), db)])
    head_tables = rope_tables(pos, ROT_DIM, HEAD_DIM)
    idx_tables = rope_tables(pos, IDX_ROT, IDX_DIM)

    outs_a = {"kp": [], "vp": [], "ks": [], "vs": []}
    outs_b = {"kp": [], "vp": [], "ip": [], "ks": [], "vs": [], "is": []}
    for i in range(depth):
        j = i // 2
        xn = rmsnorm(h, attn_norm[i], (BF16,))[0]
        if i % 2 == 0:
            proj = matmul(xn, a_w_qkv[j].astype(BF16), name="qkv_proj")
            q_b, k_f, k_b, v_f, v_b = qkv_post(proj, a_q_norm[j], a_k_norm[j], head_tables, n_heads,
                                               HEAD_DIM ** -0.5)
            o_p = swa_prompt(q_b, k_b, v_b, a_sinks[j], nb, seq, n_heads)
            k_all = jnp.concatenate([cache_a_k[j].astype(F32), k_f[tp:].reshape(db, ds, N_KV, HEAD_DIM)], axis=1)
            v_all = jnp.concatenate([cache_a_v[j].astype(F32), v_f[tp:].reshape(db, ds, N_KV, HEAD_DIM)], axis=1)
            o_s = swa_sample(q_b, k_all.reshape(db, WINDOW + ds, kvc).astype(BF16),
                             v_all.reshape(db, WINDOW + ds, kvc).astype(BF16), a_sinks[j], tp, db, ds, n_heads)
            outs_a["kp"].append(k_f[:tp].reshape(nb, seq, N_KV, HEAD_DIM)[:, -WINDOW:])
            outs_a["vp"].append(v_f[:tp].reshape(nb, seq, N_KV, HEAD_DIM)[:, -WINDOW:])
            outs_a["ks"].append(k_all[:, -WINDOW:])
            outs_a["vs"].append(v_all[:, -WINDOW:])
            w_o = a_w_o[j]
        else:
            n_qkv = qc + 2 * kvc
            n_rest = b_w_qkv.shape[2] - n_qkv
            rest_w = IDX_HEADS * IDX_DIM + LANES
            proj = matmul(xn, b_w_qkv[j][:, :n_qkv].astype(BF16), name="qkv_proj")
            w_rest = jnp.pad(b_w_qkv[j][:, n_qkv:], ((0, 0), (0, rest_w - n_rest))).astype(BF16)
            rest = matmul(xn, w_rest, tn_target=384, name="idx_proj")
            q_b, k_f, k_b, v_f, v_b = qkv_post(proj, b_q_norm[j], b_k_norm[j], head_tables, n_heads,
                                               HEAD_DIM ** -0.5 * LOG2E)
            qi_b, tail, kidx_b = idx_post(rest, b_kidx_norm[j], idx_tables)
            kidx_f = tail[:, :IDX_DIM]
            w_idx = tail[:, IDX_DIM:IDX_DIM + IDX_HEADS]

            blk_q = jnp.arange(nqb, dtype=I32)
            nkt_p = jnp.tile(((blk_q + 1) * QBLOCK + KEY_TILE - 1) // KEY_TILE, nb).astype(I32)
            nat_p = jnp.tile(((blk_q + 1) * QBLOCK + ATT_TILE - 1) // ATT_TILE, nb).astype(I32)
            qrow = blk_q[:, None] * QBLOCK + jnp.arange(QBLOCK, dtype=I32)[None, :]
            lim_p = jnp.tile(((qrow // CHUNK + 1) * CHUNK)[:, None, :], (nb, 1, 1)).astype(I32)
            qi_p = _query_blocks(qi_b[:tp], nb * nqb, IDX_DIM)
            w_p = _query_blocks(w_idx[:tp], nb * nqb, 1).reshape(nb * nqb, 1, IDX_HEADS * QBLOCK)
            bias_p = dsa_select(nkt_p, qi_p, w_p, lim_p, kidx_b, nb, seq, min(TOPK_MAX, seq // 4))
            o_p = dsa_attend(nat_p, q_b, k_b, v_b, bias_p, nb, n_heads)

            n_keys = past + ds
            lk = (n_keys + 2 * ATT_TILE - 1) // (2 * ATT_TILE) * (2 * ATT_TILE)
            pad_k = lambda new, cached: jnp.pad(
                jnp.concatenate([cached.reshape(db, past, -1), new.reshape(db, ds, -1)], axis=1),
                ((0, 0), (0, lk - n_keys), (0, 0))).reshape(db * lk, -1)
            cached_rows = lambda c: cache_rows(c.reshape(db * past, N_KV, HEAD_DIM))
            pad_q = lambda x: jnp.pad(x.reshape(db, ds, -1), ((0, 0), (0, QBLOCK - ds), (0, 0))).reshape(
                db * QBLOCK, -1)
            nkt_s = jnp.full((db,), (n_keys + KEY_TILE - 1) // KEY_TILE, I32)
            nat_s = jnp.full((db,), (n_keys + ATT_TILE - 1) // ATT_TILE, I32)
            lim_s = jnp.tile(jnp.where(jnp.arange(QBLOCK) < ds, n_keys, 0).astype(I32)[None, None, :], (db, 1, 1))
            qi_s = _query_blocks(pad_q(qi_b[tp:]), db, IDX_DIM)
            w_s = _query_blocks(pad_q(w_idx[tp:]), db, 1).reshape(db, 1, IDX_HEADS * QBLOCK)
            bias_s = dsa_select(nkt_s, qi_s, w_s, lim_s, pad_k(kidx_b[tp:], cache_b_kidx[j].astype(BF16)), db, lk,
                                min(TOPK_MAX, n_keys // 4))
            o_s = dsa_attend(nat_s, pad_q(q_b[tp:]), pad_k(k_b[tp:], cached_rows(cache_b_k[j])),
                             pad_k(v_b[tp:], cached_rows(cache_b_v[j])), bias_s, db, n_heads)
            o_s = o_s.reshape(db, QBLOCK, qc)[:, :ds].reshape(ts, qc)

            outs_b["kp"].append(k_f[:tp].reshape(nb, seq, N_KV, HEAD_DIM))
            outs_b["vp"].append(v_f[:tp].reshape(nb, seq, N_KV, HEAD_DIM))
            outs_b["ip"].append(kidx_f[:tp].reshape(nb, seq, IDX_DIM))
            outs_b["ks"].append(k_f[tp:].reshape(db, ds, N_KV, HEAD_DIM))
            outs_b["vs"].append(v_f[tp:].reshape(db, ds, N_KV, HEAD_DIM))
            outs_b["is"].append(kidx_f[tp:].reshape(db, ds, IDX_DIM))
            w_o = b_w_o[j]
        h = matmul(jnp.concatenate([o_p, o_s], axis=0), w_o.astype(BF16), res=h, name="out_proj")
        h = moe_layer(h, ffn_norm[i], moe_w_group[i], moe_b_group[i], moe_w_expert[i], moe_b_expert[i],
                      moe_w_gate[i], moe_w_up[i], moe_w_down[i], split_rows=tp if i == depth - 1 else None)

    return (h[0].reshape(nb, seq, d), h[1].reshape(db, ds, d),
            jnp.stack(outs_a["kp"]), jnp.stack(outs_a["vp"]),
            jnp.stack(outs_b["kp"]), jnp.stack(outs_b["vp"]), jnp.stack(outs_b["ip"]),
            jnp.stack(outs_a["ks"]), jnp.stack(outs_a["vs"]),
            jnp.stack(outs_b["ks"]), jnp.stack(outs_b["vs"]), jnp.stack(outs_b["is"]))
```

```python
import functools
import math

import jax
import jax.numpy as jnp
from jax import lax
from jax.experimental import pallas as pl
from jax.experimental.pallas import tpu as pltpu

F32 = jnp.float32
BF16 = jnp.bfloat16
I32 = jnp.int32

LANES = 128
CHUNK = 64
HEAD_DIM = 128
N_KV = 8
WINDOW = 128
ROT_DIM = HEAD_DIM // 4
ROPE_THETA = 500000.0
IDX_HEADS = 16
IDX_DIM = 64
IDX_ROT = IDX_DIM // 4
TOPK_MAX = 256
QBLOCK = 128
N_GROUPS = 8
EXP_PER_GROUP = 8
N_EXPERTS = N_GROUPS * EXP_PER_GROUP
RMS_EPS = 1e-6
NEG = -1e30
INT_MIN = -2 ** 31
KEY_TILE = 256
COUNT_ROWS = 1024
SEARCH_GROUP = 8
ATT_TILE = 512
LONG_TRIP = 4
HEADS_PER_STEP = 2
LOG2E = 1.4426950408889634
MOE_ROWS = 256
VMEM_LIMIT = 56 * 1024 * 1024

_NT = (((1,), (1,)), ((), ()))


def _cparams(*sem):
    return pltpu.CompilerParams(dimension_semantics=sem, vmem_limit_bytes=VMEM_LIMIT)


def _pick_tile(n, target, mult):
    best = None
    t = mult
    while t <= min(n, target):
        if n % t == 0:
            best = t
        t += mult
    return best if best is not None else n


def _rmsnorm_kernel(x_ref, g_ref, *o_refs):
    x = x_ref[...]
    y = x * lax.rsqrt(jnp.mean(x * x, axis=-1, keepdims=True) + RMS_EPS) * g_ref[...]
    for o_ref in o_refs:
        o_ref[...] = y.astype(o_ref.dtype)


def rmsnorm(x, g, out_dtypes):
    t, d = x.shape
    tr = _pick_tile(t, 256, 16)
    outs = pl.pallas_call(
        _rmsnorm_kernel,
        grid=(t // tr,),
        in_specs=[pl.BlockSpec((tr, d), lambda i: (i, 0)),
                  pl.BlockSpec((1, d), lambda i: (0, 0))],
        out_specs=[pl.BlockSpec((tr, d), lambda i: (i, 0)) for _ in out_dtypes],
        out_shape=[jax.ShapeDtypeStruct((t, d), dt) for dt in out_dtypes],
        compiler_params=_cparams("parallel"),
        name="rmsnorm",
    )(x, g.reshape(1, d).astype(F32))
    return outs


def _rmsnorm_pack_kernel(x_ref, g_ref, b_ref, p_ref):
    x = x_ref[...]
    y = (x * lax.rsqrt(jnp.mean(x * x, axis=-1, keepdims=True) + RMS_EPS) * g_ref[...]).astype(BF16)
    b_ref[...] = y
    half = y.shape[1] // 2
    hi = pltpu.bitcast(y[:, :half].astype(F32), I32)
    lo = pltpu.bitcast(y[:, half:].astype(F32), I32)
    p_ref[...] = hi | lax.shift_right_logical(lo, 16)


def rmsnorm_pack(x, g):
    t, d = x.shape
    tr = _pick_tile(t, 256, 16)
    return pl.pallas_call(
        _rmsnorm_pack_kernel,
        grid=(t // tr,),
        in_specs=[pl.BlockSpec((tr, d), lambda i: (i, 0)),
                  pl.BlockSpec((1, d), lambda i: (0, 0))],
        out_specs=[pl.BlockSpec((tr, d), lambda i: (i, 0)), pl.BlockSpec((tr, d // 2), lambda i: (i, 0))],
        out_shape=[jax.ShapeDtypeStruct((t, d), BF16), jax.ShapeDtypeStruct((t, d // 2), I32)],
        compiler_params=_cparams("parallel"),
        name="rmsnorm_pack",
    )(x, g.reshape(1, d).astype(F32))


def _unpack_rows(p):
    first = pltpu.bitcast(p & jnp.int32(-65536), F32).astype(BF16)
    second = pltpu.bitcast(lax.shift_left(p, 16), F32).astype(BF16)
    return first, second


def _mm_kernel(a_ref, b_ref, o_ref):
    o_ref[...] = jnp.dot(a_ref[...], b_ref[...], preferred_element_type=F32).astype(o_ref.dtype)


def _mm_res_kernel(a_ref, b_ref, r_ref, o_ref):
    o_ref[...] = r_ref[...] + jnp.dot(a_ref[...], b_ref[...], preferred_element_type=F32)


def matmul(a, b, res=None, tm_target=1280, tn_target=512, name="matmul"):
    m, k = a.shape
    _, n = b.shape
    tm = _pick_tile(m, tm_target, 16)
    tn = _pick_tile(n, tn_target, LANES)
    in_specs = [pl.BlockSpec((tm, k), lambda i, j: (i, 0)),
                pl.BlockSpec((k, tn), lambda i, j: (0, j))]
    args = [a, b]
    if res is not None:
        in_specs.append(pl.BlockSpec((tm, tn), lambda i, j: (i, j)))
        args.append(res)
    return pl.pallas_call(
        _mm_kernel if res is None else _mm_res_kernel,
        grid=(m // tm, n // tn),
        in_specs=in_specs,
        out_specs=pl.BlockSpec((tm, tn), lambda i, j: (i, j)),
        out_shape=jax.ShapeDtypeStruct((m, n), F32),
        compiler_params=_cparams("parallel", "arbitrary"),
        name=name,
    )(*args)


def rope_tables(pos, rot_dim, period):
    half = rot_dim // 2
    inv_freq = 1.0 / (ROPE_THETA ** (jnp.arange(half, dtype=F32) * (2.0 / rot_dim)))
    ang = pos.astype(F32)[:, None] * inv_freq[None, :]
    cos, sin = jnp.cos(ang), jnp.sin(ang)
    t = pos.shape[0]
    rest = period - rot_dim
    c = jnp.concatenate([cos, cos, jnp.ones((t, rest), F32)], axis=1)
    s1 = jnp.concatenate([-sin, jnp.zeros((t, half + rest), F32)], axis=1)
    s2 = jnp.concatenate([jnp.zeros((t, half), F32), sin, jnp.zeros((t, rest), F32)], axis=1)
    rep = LANES // period
    return tuple(jnp.tile(x, (1, rep)) for x in (c, s1, s2))


def _rope(y, c, s1, s2, half):
    return y * c + pltpu.roll(y, LANES - half, 1) * s1 + pltpu.roll(y, half, 1) * s2


def _qkv_post_kernel(p_ref, qg_ref, kg_ref, c_ref, s1_ref, s2_ref,
                     q_ref, kf_ref, kb_ref, vf_ref, vb_ref, *, n_heads, q_scale):
    c, s1, s2 = c_ref[...], s1_ref[...], s2_ref[...]

    def normrope(x, g):
        y = x * lax.rsqrt(jnp.mean(x * x, axis=-1, keepdims=True) + RMS_EPS) * g
        return _rope(y, c, s1, s2, ROT_DIM // 2)

    qg, kg = qg_ref[...], kg_ref[...]
    for h in range(n_heads):
        sl = slice(h * HEAD_DIM, (h + 1) * HEAD_DIM)
        q_ref[:, sl] = (normrope(p_ref[:, sl], qg) * q_scale).astype(BF16)
    for h in range(N_KV):
        sl = slice(h * HEAD_DIM, (h + 1) * HEAD_DIM)
        k = normrope(p_ref[:, (n_heads + h) * HEAD_DIM:(n_heads + h + 1) * HEAD_DIM], kg)
        kf_ref[:, h, :] = k
        kb_ref[:, sl] = k.astype(BF16)
        v = p_ref[:, (n_heads + N_KV + h) * HEAD_DIM:(n_heads + N_KV + h + 1) * HEAD_DIM]
        vf_ref[:, h, :] = v
        vb_ref[:, sl] = v.astype(BF16)


def qkv_post(proj, q_gain, k_gain, tables, n_heads, q_scale):
    t, ncol = proj.shape
    tr = _pick_tile(t, 256, 16)
    kvc = N_KV * HEAD_DIM
    qc = n_heads * HEAD_DIM
    row = lambda w: pl.BlockSpec((tr, w), lambda i: (i, 0))
    row3 = pl.BlockSpec((tr, N_KV, HEAD_DIM), lambda i: (i, 0, 0))
    vec = pl.BlockSpec((1, HEAD_DIM), lambda i: (0, 0))
    cache = jax.ShapeDtypeStruct((t, N_KV, HEAD_DIM), F32)
    return pl.pallas_call(
        functools.partial(_qkv_post_kernel, n_heads=n_heads, q_scale=q_scale),
        grid=(t // tr,),
        in_specs=[row(ncol), vec, vec, row(LANES), row(LANES), row(LANES)],
        out_specs=[row(qc), row3, row(kvc), row3, row(kvc)],
        out_shape=[jax.ShapeDtypeStruct((t, qc), BF16),
                   cache, jax.ShapeDtypeStruct((t, kvc), BF16),
                   cache, jax.ShapeDtypeStruct((t, kvc), BF16)],
        compiler_params=_cparams("parallel"),
        name="qkv_post",
    )(proj, q_gain.reshape(1, HEAD_DIM).astype(F32), k_gain.reshape(1, HEAD_DIM).astype(F32), *tables)


def _cache_rows_kernel(x_ref, o_ref):
    for h in range(N_KV):
        o_ref[:, h * HEAD_DIM:(h + 1) * HEAD_DIM] = x_ref[:, h, :].astype(BF16)


def cache_rows(c):
    r = c.shape[0]
    tr = _pick_tile(r, 256, 16)
    return pl.pallas_call(
        _cache_rows_kernel,
        grid=(r // tr,),
        in_specs=[pl.BlockSpec((tr, N_KV, HEAD_DIM), lambda i: (i, 0, 0))],
        out_specs=pl.BlockSpec((tr, N_KV * HEAD_DIM), lambda i: (i, 0)),
        out_shape=jax.ShapeDtypeStruct((r, N_KV * HEAD_DIM), BF16),
        compiler_params=_cparams("parallel"),
        name="cache_rows",
    )(c)


IDX_W_SCALE = (IDX_HEADS ** -0.5) * (IDX_DIM ** -0.5)


def _idx_post_kernel(r_ref, g_ref, c_ref, s1_ref, s2_ref, qi_ref, tail_ref, kb_ref):
    c, s1, s2 = c_ref[...], s1_ref[...], s2_ref[...]
    nq = IDX_HEADS * IDX_DIM
    for j in range(nq // LANES):
        sl = slice(j * LANES, (j + 1) * LANES)
        qi_ref[:, sl] = _rope(r_ref[:, sl], c, s1, s2, IDX_ROT // 2).astype(BF16)
    t = r_ref[:, nq:nq + LANES]
    lane = lax.broadcasted_iota(I32, t.shape, 1)
    is_k = lane < IDX_DIM
    ms = jnp.sum(jnp.where(is_k, t * t, 0.0), axis=-1, keepdims=True) * (1.0 / IDX_DIM)
    kn = t * lax.rsqrt(ms + RMS_EPS) * g_ref[...]
    tail = jnp.where(is_k, _rope(kn, c, s1, s2, IDX_ROT // 2), t * IDX_W_SCALE)
    tail_ref[...] = tail
    kb_ref[...] = tail[:, :IDX_DIM].astype(BF16)


def idx_post(rest, kidx_gain, tables):
    t, ncol = rest.shape
    tr = _pick_tile(t, 256, 16)
    nq = IDX_HEADS * IDX_DIM
    row = lambda w: pl.BlockSpec((tr, w), lambda i: (i, 0))
    g = jnp.concatenate([kidx_gain.astype(F32), jnp.zeros((LANES - IDX_DIM,), F32)]).reshape(1, LANES)
    return pl.pallas_call(
        _idx_post_kernel,
        grid=(t // tr,),
        in_specs=[row(ncol), pl.BlockSpec((1, LANES), lambda i: (0, 0)), row(LANES), row(LANES), row(LANES)],
        out_specs=[row(nq), row(LANES), row(IDX_DIM)],
        out_shape=[jax.ShapeDtypeStruct((t, nq), BF16), jax.ShapeDtypeStruct((t, LANES), F32),
                   jax.ShapeDtypeStruct((t, IDX_DIM), BF16)],
        compiler_params=_cparams("parallel"),
        name="idx_post",
    )(rest, g, *tables)


def _sink_attn(q, k, v, bias, sink):
    s = lax.dot_general(q, k, _NT, preferred_element_type=F32)
    if bias is not None:
        s = s + bias
    m = jnp.maximum(jnp.max(s, axis=-1, keepdims=True), sink)
    p = jnp.exp(s - m)
    denom = jnp.sum(p, axis=-1, keepdims=True) + jnp.exp(sink - m)
    o = jnp.dot(p.astype(BF16), v, preferred_element_type=F32)
    return o / denom


def _swa_prompt_kernel(sink_ref, q_ref, kp_ref, kc_ref, vp_ref, vc_ref, o_ref, *, group):
    i = pl.program_id(1)
    rows, cols = QBLOCK, 2 * QBLOCK
    qc = lax.broadcasted_iota(I32, (rows, cols), 0) // CHUNK
    kc = lax.broadcasted_iota(I32, (rows, cols), 1) // CHUNK
    first_valid = jnp.where(i > 0, 0, QBLOCK // CHUNK)
    vis = (kc >= qc) & (kc <= qc + WINDOW // CHUNK) & (kc >= first_valid)
    bias = jnp.where(vis, 0.0, NEG)
    for kv in range(N_KV):
        sl = slice(kv * HEAD_DIM, (kv + 1) * HEAD_DIM)
        k = jnp.concatenate([kp_ref[:, sl], kc_ref[:, sl]], axis=0)
        v = jnp.concatenate([vp_ref[:, sl], vc_ref[:, sl]], axis=0)
        for g in range(group):
            h = kv * group + g
            hs = slice(h * HEAD_DIM, (h + 1) * HEAD_DIM)
            o_ref[:, hs] = _sink_attn(q_ref[:, hs], k, v, bias, sink_ref[h]).astype(BF16)


def swa_prompt(q, k, v, sinks, n_batch, seq, n_heads):
    nqb = seq // QBLOCK
    qc = n_heads * HEAD_DIM
    kvc = N_KV * HEAD_DIM
    cur = lambda w: pl.BlockSpec((QBLOCK, w), lambda b, i: (b * nqb + i, 0))
    prev = lambda w: pl.BlockSpec((QBLOCK, w), lambda b, i: (b * nqb + jnp.maximum(i - 1, 0), 0))
    return pl.pallas_call(
        functools.partial(_swa_prompt_kernel, group=n_heads // N_KV),
        grid=(n_batch, nqb),
        in_specs=[pl.BlockSpec(memory_space=pltpu.SMEM),
                  cur(qc), prev(kvc), cur(kvc), prev(kvc), cur(kvc)],
        out_specs=cur(qc),
        out_shape=jax.ShapeDtypeStruct((n_batch * seq, qc), BF16),
        compiler_params=_cparams("parallel", "arbitrary"),
        name="swa_prompt",
    )(sinks.astype(F32), q, k, k, v, v)


def _swa_sample_kernel(sink_ref, q_ref, k_ref, v_ref, o_ref, *, group):
    for kv in range(N_KV):
        sl = slice(kv * HEAD_DIM, (kv + 1) * HEAD_DIM)
        k, v = k_ref[:, sl], v_ref[:, sl]
        for g in range(group):
            h = kv * group + g
            hs = slice(h * HEAD_DIM, (h + 1) * HEAD_DIM)
            o_ref[:, hs] = _sink_attn(q_ref[:, hs], k, v, None, sink_ref[h]).astype(BF16)


def swa_sample(q, k_all, v_all, sinks, row0, n_batch, n_new, n_heads):
    qc = n_heads * HEAD_DIM
    nk = k_all.shape[1]
    kvc = N_KV * HEAD_DIM
    return pl.pallas_call(
        functools.partial(_swa_sample_kernel, group=n_heads // N_KV),
        grid=(n_batch,),
        in_specs=[pl.BlockSpec(memory_space=pltpu.SMEM),
                  pl.BlockSpec((n_new, qc), lambda b: (row0 // n_new + b, 0)),
                  pl.BlockSpec((None, nk, kvc), lambda b: (b, 0, 0)),
                  pl.BlockSpec((None, nk, kvc), lambda b: (b, 0, 0))],
        out_specs=pl.BlockSpec((n_new, qc), lambda b: (b, 0)),
        out_shape=jax.ShapeDtypeStruct((n_batch * n_new, qc), BF16),
        compiler_params=_cparams("parallel"),
        name="swa_sample",
    )(sinks.astype(F32), q, k_all, v_all)


def _select_kernel(nkt_ref, qi_ref, w_ref, lim_ref, kidx_ref, bias_ref, keys_ref, p_ref,
                   *, topk, n_tiles, idx_bits):
    tk = KEY_TILE
    blk = pl.program_id(0) * pl.num_programs(1) + pl.program_id(1)
    nkt = nkt_ref[blk]
    qi = qi_ref[...]
    w = w_ref[...]
    lim = lim_ref[...]
    imin = jnp.int32(INT_MIN)

    def tile_rows(kt):
        return pl.ds(pl.multiple_of(kt * tk, tk), tk)

    def key_index(kt):
        return kt * tk + lax.broadcasted_iota(I32, (tk, LANES), 0)

    def score_tile(kt, carry):
        d = lax.dot_general(kidx_ref[tile_rows(kt), :], qi, _NT, preferred_element_type=F32)
        r = jnp.maximum(d, 0.0) * w
        acc = r[:, 0:LANES]
        for h in range(1, IDX_HEADS):
            acc = acc + r[:, h * LANES:(h + 1) * LANES]
        bits = pltpu.bitcast(acc, I32)
        key = bits ^ ((bits >> 31) & jnp.int32(0x7FFFFFFF))
        keys_ref[tile_rows(kt), :] = jnp.where(key_index(kt) < lim, key, imin)
        return carry

    lax.fori_loop(0, nkt, score_tile, 0)

    per_trip = COUNT_ROWS // tk
    n_trips = (nkt + per_trip - 1) // per_trip

    def inadmissible_tile(kt, carry):
        keys_ref[tile_rows(kt), :] = jnp.full((tk, LANES), imin, I32)
        return carry

    lax.fori_loop(nkt, n_trips * per_trip, inadmissible_tile, 0)

    def count(pred):
        def body(j, acc):
            rows = pl.ds(pl.multiple_of(j * COUNT_ROWS, COUNT_ROWS), COUNT_ROWS)
            index = j * COUNT_ROWS + lax.broadcasted_iota(I32, (COUNT_ROWS, LANES), 0)
            c = jnp.where(pred(keys_ref[rows, :], index), 1, 0).astype(I32)
            parts = [c[r * 8:(r + 1) * 8, :] for r in range(COUNT_ROWS // 8)]
            while len(parts) > 1:
                parts = [parts[a] + parts[a + 1] for a in range(0, len(parts), 2)]
            return acc + parts[0]
        acc = lax.fori_loop(0, n_trips, body, jnp.zeros((8, LANES), I32))
        return jnp.sum(acc, axis=0, keepdims=True)

    def search_bit(i, state):
        v, settled = state
        cand = v + lax.shift_left(jnp.int32(1), 31 - i)
        cnt = count(lambda kk, _: kk >= cand)
        v = jnp.where((settled == 0) & (cnt >= topk), cand, v)
        return v, jnp.where(cnt == topk, 1, settled)

    def search_group(state):
        g, v, settled, _ = state
        v, settled = lax.fori_loop(g * SEARCH_GROUP, (g + 1) * SEARCH_GROUP, search_bit, (v, settled))
        return g + 1, v, settled, jnp.min(settled)

    settled0 = jnp.where(lim < topk, 1, 0).astype(I32)
    _, tau, _, _ = lax.while_loop(lambda s: (s[0] < 32 // SEARCH_GROUP) & (s[3] == 0), search_group,
                                  (jnp.int32(0), jnp.full((1, LANES), imin, I32), settled0, jnp.min(settled0)))

    n_ge = count(lambda kk, _: (kk >= tau) & (kk != imin))
    n_gt = count(lambda kk, _: kk > tau)
    need = topk - n_gt

    p_ref[...] = jnp.full((1, LANES), 2 ** 30, I32)

    @pl.when(jnp.max(n_ge) > topk)
    def _():
        def index_bit(i, p):
            cand = p + lax.shift_left(jnp.int32(1), idx_bits - 1 - i)
            cnt = count(lambda kk, si: (kk == tau) & (si < cand))
            return jnp.where(cnt < need, cand, p)
        p_ref[...] = lax.fori_loop(0, idx_bits, index_bit, jnp.zeros((1, LANES), I32))

    p_last = p_ref[...]

    def write_tile(kt, carry):
        kk = keys_ref[tile_rows(kt), :]
        sel = (kk > tau) | ((kk == tau) & (key_index(kt) <= p_last) & (kk != imin))
        bias_ref[tile_rows(kt), :] = jnp.where(sel, 0.0, NEG).astype(BF16)
        return carry

    lax.fori_loop(0, nkt, write_tile, 0)

    def fill_tile(kt, carry):
        bias_ref[tile_rows(kt), :] = jnp.full((tk, LANES), NEG, BF16)
        return carry

    lax.fori_loop(nkt, n_tiles, fill_tile, 0)


def dsa_select(nkt, qi_blocks, w_blocks, limits, kidx, nb, lk, topk):
    nq = qi_blocks.shape[0] // nb
    assert lk % COUNT_ROWS == 0, "the counting passes walk whole COUNT_ROWS-row trips"
    n_tiles = lk // KEY_TILE
    idx_bits = max(1, int(lk).bit_length())
    per_block = lambda shape: pl.BlockSpec((None,) + shape, lambda b, q, *_: (b * nq + q, 0, 0))
    grid_spec = pltpu.PrefetchScalarGridSpec(
        num_scalar_prefetch=1,
        grid=(nb, nq),
        in_specs=[per_block((IDX_HEADS * QBLOCK, IDX_DIM)),
                  per_block((1, IDX_HEADS * QBLOCK)),
                  per_block((1, LANES)),
                  pl.BlockSpec((lk, IDX_DIM), lambda b, q, *_: (b, 0))],
        out_specs=per_block((lk, LANES)),
        scratch_shapes=[pltpu.VMEM((lk, LANES), I32), pltpu.VMEM((1, LANES), I32)],
    )
    return pl.pallas_call(
        functools.partial(_select_kernel, topk=topk, n_tiles=n_tiles, idx_bits=idx_bits),
        grid_spec=grid_spec,
        out_shape=jax.ShapeDtypeStruct((nb * nq, lk, LANES), BF16),
        compiler_params=_cparams("parallel", "arbitrary"),
        name="dsa_select",
    )(nkt, qi_blocks, w_blocks, limits, kidx)


def _masked_attn_kernel(nkt_ref, q_ref, k_ref, v_ref, bias_ref, o_ref, m_ref, acc_ref, s0_ref, s1_ref,
                        *, group, n_tiles):
    tk = ATT_TILE
    blk = pl.program_id(0) * pl.num_programs(2) + pl.program_id(2)
    nkt = nkt_ref[blk]
    rows = group * QBLOCK
    gw = group * HEAD_DIM
    heads = range(HEADS_PER_STEP)
    eye = jnp.where(lax.broadcasted_iota(I32, (QBLOCK, QBLOCK), 0)
                    == lax.broadcasted_iota(I32, (QBLOCK, QBLOCK), 1), 1.0, 0.0).astype(BF16)
    q_aug = [jnp.concatenate(
        [jnp.concatenate([q_ref[:, hh * gw + g * HEAD_DIM:hh * gw + (g + 1) * HEAD_DIM], eye], axis=1)
         for g in range(group)], axis=0) for hh in heads]
    ones = jnp.ones((tk, HEAD_DIM), BF16)
    m_ref[...] = jnp.full(m_ref.shape, NEG, F32)
    acc_ref[...] = jnp.zeros(acc_ref.shape, F32)

    def tile_rows(kt):
        return pl.ds(pl.multiple_of(jnp.minimum(kt, n_tiles - 1) * tk, tk), tk)

    def head_cols(hh):
        return slice(hh * HEAD_DIM, (hh + 1) * HEAD_DIM)

    def scores(s_ref, kt):
        sl = tile_rows(kt)
        bias = bias_ref[sl, :]
        for hh in heads:
            k_aug = jnp.concatenate([k_ref[sl, head_cols(hh)], bias], axis=1)
            s_ref[hh] = lax.dot_general(q_aug[hh], k_aug, _NT, preferred_element_type=F32)

    def update(s_ref, kt):
        sl = tile_rows(kt)
        for hh in heads:
            s = s_ref[hh]
            m_prev = m_ref[hh]
            m_new = jnp.maximum(m_prev, jnp.max(s, axis=-1, keepdims=True))
            alpha = jnp.exp2(m_prev - m_new)
            p = jnp.exp2(s - jnp.concatenate([m_new] * (tk // LANES), axis=1))
            v_aug = jnp.concatenate([v_ref[sl, head_cols(hh)], ones], axis=1)
            pv = jnp.dot(p.astype(BF16), v_aug, preferred_element_type=F32)
            acc_ref[hh] = jnp.concatenate([alpha, alpha], axis=1) * acc_ref[hh] + pv
            m_ref[hh] = m_new

    scores(s0_ref, 0)

    def run(first, n_trips, per_trip):
        def body(u, carry):
            base = first + per_trip * u
            for i in range(0, per_trip, 2):
                scores(s1_ref, base + i + 1)
                update(s0_ref, base + i)
                scores(s0_ref, base + i + 2)
                update(s1_ref, base + i + 1)
            return carry
        lax.fori_loop(0, n_trips, body, 0)

    n_long = nkt // LONG_TRIP
    run(0, n_long, LONG_TRIP)
    run(n_long * LONG_TRIP, (nkt - n_long * LONG_TRIP + 1) // 2, 2)
    for hh in heads:
        acc = acc_ref[hh]
        o = acc[:, :HEAD_DIM] / jnp.maximum(acc[:, HEAD_DIM:], 1e-30)
        for g in range(group):
            o_ref[:, hh * gw + g * HEAD_DIM:hh * gw + (g + 1) * HEAD_DIM] = (
                o[g * QBLOCK:(g + 1) * QBLOCK].astype(BF16))


def dsa_attend(nkt, q, k, v, bias, nb, n_heads):
    lk = bias.shape[1]
    nq = bias.shape[0] // nb
    group = n_heads // N_KV
    gw = group * HEAD_DIM
    rows = group * QBLOCK
    n_tiles = lk // ATT_TILE
    assert lk % (2 * ATT_TILE) == 0, "the tile-pair loop needs an even number of key tiles"
    hps = HEADS_PER_STEP
    qspec = pl.BlockSpec((QBLOCK, hps * gw), lambda b, h, q, *_: (b * nq + q, h))
    kvspec = pl.BlockSpec((lk, hps * HEAD_DIM), lambda b, h, q, *_: (b, h))
    grid_spec = pltpu.PrefetchScalarGridSpec(
        num_scalar_prefetch=1,
        grid=(nb, N_KV // hps, nq),
        in_specs=[qspec, kvspec, kvspec,
                  pl.BlockSpec((None, lk, LANES), lambda b, h, q, *_: (b * nq + q, 0, 0))],
        out_specs=qspec,
        scratch_shapes=[pltpu.VMEM((hps, rows, LANES), F32),
                        pltpu.VMEM((hps, rows, 2 * HEAD_DIM), F32),
                        pltpu.VMEM((hps, rows, ATT_TILE), F32),
                        pltpu.VMEM((hps, rows, ATT_TILE), F32)],
    )
    return pl.pallas_call(
        functools.partial(_masked_attn_kernel, group=group, n_tiles=n_tiles),
        grid_spec=grid_spec,
        out_shape=jax.ShapeDtypeStruct((nb * nq * QBLOCK, q.shape[1]), BF16),
        compiler_params=_cparams("parallel", "parallel", "arbitrary"),
        name="dsa_attend",
    )(nkt, q, k, v, bias)


def _router_kernel(l_ref, b_ref, id_ref, gate_ref):
    x = l_ref[...] + b_ref[...]
    lane = lax.broadcasted_iota(I32, x.shape, 1)
    big = jnp.int32(2 ** 30)
    ninf = -jnp.inf

    def first_max(vals):
        m = jnp.max(vals, axis=-1, keepdims=True)
        return m, jnp.min(jnp.where(vals == m, lane, big), axis=-1, keepdims=True)

    glog = jnp.where(lane < N_GROUPS, x, ninf)
    gmax, gsel = first_max(glog)
    gprob = 1.0 / jnp.sum(jnp.exp(glog - gmax), axis=-1, keepdims=True)
    in_group = (lane >= N_GROUPS) & ((lane - N_GROUPS) // EXP_PER_GROUP == gsel) & (lane < N_GROUPS + N_EXPERTS)
    e1 = jnp.where(in_group, x, ninf)
    v1, i1 = first_max(e1)
    v2, i2 = first_max(jnp.where(lane == i1, ninf, e1))
    t = jnp.exp(v2 - v1)
    g1 = gprob / (1.0 + t)
    g2 = gprob * t / (1.0 + t)
    id_ref[...] = jnp.where(lane == 0, i1 - N_GROUPS, jnp.where(lane == 1, i2 - N_GROUPS, 0))
    gate_ref[...] = jnp.where(lane == 0, g1, jnp.where(lane == 1, g2, 0.0))


def router(logits, bias):
    t = logits.shape[0]
    tr = _pick_tile(t, 256, 8)
    row = pl.BlockSpec((tr, LANES), lambda i: (i, 0))
    return pl.pallas_call(
        _router_kernel,
        grid=(t // tr,),
        in_specs=[row, pl.BlockSpec((1, LANES), lambda i: (0, 0))],
        out_specs=[row, row],
        out_shape=[jax.ShapeDtypeStruct((t, LANES), I32), jax.ShapeDtypeStruct((t, LANES), F32)],
        compiler_params=_cparams("parallel"),
        name="router",
    )(logits, bias)


DMA_UNROLL = 8


def _row_copy(src_hbm, row, dst_ref, r, sem):
    return pltpu.make_async_copy(src_hbm.at[pl.ds(row, 1), :], dst_ref.at[pl.ds(r, 1), :], sem)


def _start_row_gather(src_hbm, row_of, dst_ref, sem, n_rows, beside_block_loads=False):
    def trip(j, carry):
        for u in range(DMA_UNROLL):
            r = j * DMA_UNROLL + u
            _row_copy(src_hbm, row_of(r), dst_ref, r, sem).start(priority=1 if beside_block_loads else u % 2)
        return carry
    lax.fori_loop(0, n_rows // DMA_UNROLL, trip, 0)


def _wait_row_gather(src_hbm, dst_ref, sem, n_rows):
    pltpu.make_async_copy(src_hbm.at[pl.ds(0, n_rows), :], dst_ref, sem).wait()


def _experts_kernel(tok_ref, bexp_ref, nused_ref, x_hbm, wg_ref, wu_ref, wd_ref, o_ref, xbuf_ref, sem_ref):
    i, j = pl.program_id(0), pl.program_id(1)
    n_used = nused_ref[0]
    slot = i % 2
    first = j == 0

    def start_block(block, buf):
        base = block * MOE_ROWS
        _start_row_gather(x_hbm, lambda r: tok_ref[base + r], xbuf_ref.at[buf], sem_ref.at[buf], MOE_ROWS,
                          beside_block_loads=True)

    @pl.when(first & (i == 0))
    def _():
        start_block(0, 0)

    @pl.when(first & (i + 1 < n_used))
    def _():
        start_block(i + 1, 1 - slot)

    @pl.when(first & (i < n_used))
    def _():
        _wait_row_gather(x_hbm, xbuf_ref.at[slot], sem_ref.at[slot], MOE_ROWS)

    @pl.when(i < n_used)
    def _():
        x0, x1 = _unpack_rows(xbuf_ref[slot])
        half = x0.shape[1]

        def proj(w_ref):
            return (jnp.dot(x0, w_ref[:half, :].astype(BF16), preferred_element_type=F32)
                    + jnp.dot(x1, w_ref[half:, :].astype(BF16), preferred_element_type=F32))

        a = proj(wg_ref)
        b = proj(wu_ref)
        hid = (a * (1.0 / (1.0 + jnp.exp(-a)))) * b
        y = jnp.dot(hid.astype(BF16), wd_ref[...].astype(BF16), preferred_element_type=F32)

        @pl.when(first)
        def _():
            o_ref[...] = y

        @pl.when(jnp.logical_not(first))
        def _():
            o_ref[...] += y

    @pl.when(first & (i >= n_used))
    def _():
        o_ref[...] = jnp.zeros(o_ref.shape, o_ref.dtype)


def grouped_experts(tok_buf, blk_exp, n_used, x, w_gate, w_up, w_down, layer, n_blocks):
    d, f = w_gate.shape[2], w_gate.shape[3]
    fh = f // 2

    def expert_of(i, be, nu):
        return be[jnp.minimum(i, nu[0] - 1)]

    def half_of(i, j, nu):
        ii = jnp.minimum(i, nu[0] - 1)
        jj = jnp.where(i < nu[0], j, 1)
        return jnp.where(ii % 2 == 0, jj, 1 - jj)

    w_in = pl.BlockSpec((None, None, d, fh),
                        lambda i, j, tok, be, nu: (layer, expert_of(i, be, nu), 0, half_of(i, j, nu)))
    w_out = pl.BlockSpec((None, None, fh, d),
                         lambda i, j, tok, be, nu: (layer, expert_of(i, be, nu), half_of(i, j, nu), 0))
    grid_spec = pltpu.PrefetchScalarGridSpec(
        num_scalar_prefetch=3,
        grid=(n_blocks, 2),
        in_specs=[pl.BlockSpec(memory_space=pl.ANY), w_in, w_in, w_out],
        out_specs=pl.BlockSpec((MOE_ROWS, d), lambda i, j, *_: (i, 0)),
        scratch_shapes=[pltpu.VMEM((2, MOE_ROWS, x.shape[1]), x.dtype), pltpu.SemaphoreType.DMA((2,))],
    )
    return pl.pallas_call(
        _experts_kernel,
        grid_spec=grid_spec,
        out_shape=jax.ShapeDtypeStruct((n_blocks * MOE_ROWS, d), F32),
        compiler_params=_cparams("arbitrary", "arbitrary"),
        name="moe_experts",
    )(tok_buf, blk_exp, n_used, x, w_gate, w_up, w_down)


def _combine_kernel(p0_ref, p1_ref, h_ref, g_ref, y_hbm, *rest, rows, split_step):
    o_refs, (buf_ref, sem_ref) = rest[:-2], rest[-2:]
    i = pl.program_id(0)
    base = i * rows
    _start_row_gather(y_hbm, lambda r: p0_ref[base + r], buf_ref.at[0], sem_ref.at[0], rows)
    _start_row_gather(y_hbm, lambda r: p1_ref[base + r], buf_ref.at[1], sem_ref.at[1], rows)
    g = g_ref[...]
    g0, g1 = g[:, 0:1], g[:, 1:2]
    _wait_row_gather(y_hbm, buf_ref.at[0], sem_ref.at[0], rows)
    _wait_row_gather(y_hbm, buf_ref.at[1], sem_ref.at[1], rows)
    out = h_ref[...] + (g0 * buf_ref[0] + g1 * buf_ref[1])
    if split_step is None:
        o_refs[0][...] = out
    else:
        @pl.when(i < split_step)
        def _():
            o_refs[0][...] = out

        @pl.when(i >= split_step)
        def _():
            o_refs[1][...] = out


def combine(pos0, pos1, h, gates, y_sorted, split_rows=None):
    t, d = h.shape
    rows = _pick_tile(t if split_rows is None else math.gcd(split_rows, t - split_rows), 128, 8)
    row = lambda w: pl.BlockSpec((rows, w), lambda i, *_: (i, 0))
    if split_rows is None:
        split_step = None
        out_specs = row(d)
        out_shape = jax.ShapeDtypeStruct((t, d), F32)
    else:
        split_step = split_rows // rows
        out_specs = [pl.BlockSpec((rows, d), lambda i, *_: (jnp.minimum(i, split_step - 1), 0)),
                     pl.BlockSpec((rows, d), lambda i, *_: (jnp.maximum(i - split_step, 0), 0))]
        out_shape = [jax.ShapeDtypeStruct((split_rows, d), F32), jax.ShapeDtypeStruct((t - split_rows, d), F32)]
    grid_spec = pltpu.PrefetchScalarGridSpec(
        num_scalar_prefetch=2,
        grid=(t // rows,),
        in_specs=[row(d), row(LANES), pl.BlockSpec(memory_space=pl.ANY)],
        out_specs=out_specs,
        scratch_shapes=[pltpu.VMEM((2, rows, d), F32), pltpu.SemaphoreType.DMA((2,))],
    )
    return pl.pallas_call(
        functools.partial(_combine_kernel, rows=rows, split_step=split_step),
        grid_spec=grid_spec,
        out_shape=out_shape,
        compiler_params=_cparams("arbitrary"),
        name="moe_combine",
    )(pos0, pos1, h, gates, y_sorted)


def _dispatch_plan(expert, n_tokens):
    a = expert.shape[0]
    order = jnp.argsort(expert, stable=True).astype(I32)
    rank = jnp.argsort(order).astype(I32)
    e_s = expert[order]
    experts = jnp.arange(N_EXPERTS, dtype=I32)
    start = jnp.searchsorted(e_s, experts, side='left').astype(I32)
    counts = jnp.searchsorted(e_s, experts, side='right').astype(I32) - start
    pcounts = (counts + MOE_ROWS - 1) // MOE_ROWS * MOE_ROWS
    pend = jnp.cumsum(pcounts)
    pstart = pend - pcounts
    slot = pstart[expert] + (rank - start[expert])
    n_blocks = (a + MOE_ROWS - 1) // MOE_ROWS + N_EXPERTS
    blk_exp = jnp.minimum(jnp.searchsorted(pend, jnp.arange(n_blocks, dtype=I32) * MOE_ROWS, side='right'),
                          N_EXPERTS - 1).astype(I32)
    slot_exp = jnp.repeat(blk_exp, MOE_ROWS)
    within = jnp.arange(n_blocks * MOE_ROWS, dtype=I32) - pstart[slot_exp]
    src = order[jnp.clip(start[slot_exp] + within, 0, a - 1)]
    tok_buf = jnp.where((within >= 0) & (within < counts[slot_exp]), src // 2, 0).astype(I32)
    n_used = (pend[-1] // MOE_ROWS).astype(I32).reshape(1)
    return tok_buf, slot.astype(I32), blk_exp, n_used, n_blocks


def moe_layer(h, norm_g, w_group, b_group, w_expert, b_expert, w_gate, w_up, w_down, layer, split_rows=None):
    t, d = h.shape
    xn_b, xn_p = rmsnorm_pack(h, norm_g)
    n_logit = N_GROUPS + N_EXPERTS
    w_r = jnp.concatenate([w_group, w_expert, jnp.zeros((d, LANES - n_logit), F32)], axis=1).astype(BF16)
    b_r = jnp.concatenate([b_group.astype(F32), b_expert.astype(F32),
                           jnp.zeros((LANES - n_logit,), F32)]).reshape(1, LANES)
    ids, gates = router(matmul(xn_b, w_r, name="router_logits"), b_r)
    tok_buf, slot, blk_exp, n_used, n_blocks = _dispatch_plan(ids[:, :2].reshape(-1), t)
    y_sorted = grouped_experts(tok_buf, blk_exp, n_used, xn_p, w_gate, w_up, w_down, layer, n_blocks)
    slot2 = slot.reshape(t, 2)
    return combine(slot2[:, 0], slot2[:, 1], h, gates, y_sorted, split_rows)


def _query_blocks(x, n_blocks, width):
    return x.reshape(n_blocks, QBLOCK, IDX_HEADS, width).transpose(0, 2, 1, 3).reshape(
        n_blocks, IDX_HEADS * QBLOCK, width)


def kernel(x_prompt, x_sample, cache_a_k, cache_a_v, cache_b_k, cache_b_v, cache_b_kidx, attn_norm, ffn_norm, a_w_qkv, a_w_o, a_q_norm, a_k_norm, a_sinks, b_w_qkv, b_w_o, b_q_norm, b_k_norm, b_kidx_norm, moe_w_group, moe_b_group, moe_w_expert, moe_b_expert, moe_w_gate, moe_w_up, moe_w_down):
    nb, seq, d = x_prompt.shape
    db, ds, _ = x_sample.shape
    depth = attn_norm.shape[0]
    n_heads = d // HEAD_DIM
    qc = n_heads * HEAD_DIM
    kvc = N_KV * HEAD_DIM
    tp, ts = nb * seq, db * ds
    past = cache_b_k.shape[2]
    nqb = seq // QBLOCK

    h = jnp.concatenate([x_prompt.reshape(tp, d), x_sample.reshape(ts, d)], axis=0)
    pos = jnp.concatenate([jnp.tile(jnp.arange(seq), nb), jnp.tile(past + jnp.arange(ds), db)])
    head_tables = rope_tables(pos, ROT_DIM, HEAD_DIM)
    idx_tables = rope_tables(pos, IDX_ROT, IDX_DIM)

    outs_a = {"kp": [], "vp": [], "ks": [], "vs": []}
    outs_b = {"kp": [], "vp": [], "ip": [], "ks": [], "vs": [], "is": []}
    for i in range(depth):
        j = i // 2
        xn = rmsnorm(h, attn_norm[i], (BF16,))[0]
        if i % 2 == 0:
            proj = matmul(xn, a_w_qkv[j].astype(BF16), name="qkv_proj")
            q_b, k_f, k_b, v_f, v_b = qkv_post(proj, a_q_norm[j], a_k_norm[j], head_tables, n_heads,
                                               HEAD_DIM ** -0.5)
            o_p = swa_prompt(q_b, k_b, v_b, a_sinks[j], nb, seq, n_heads)
            k_all = jnp.concatenate([cache_a_k[j].astype(F32), k_f[tp:].reshape(db, ds, N_KV, HEAD_DIM)], axis=1)
            v_all = jnp.concatenate([cache_a_v[j].astype(F32), v_f[tp:].reshape(db, ds, N_KV, HEAD_DIM)], axis=1)
            o_s = swa_sample(q_b, k_all.reshape(db, WINDOW + ds, kvc).astype(BF16),
                             v_all.reshape(db, WINDOW + ds, kvc).astype(BF16), a_sinks[j], tp, db, ds, n_heads)
            outs_a["kp"].append(k_f[:tp].reshape(nb, seq, N_KV, HEAD_DIM)[:, -WINDOW:])
            outs_a["vp"].append(v_f[:tp].reshape(nb, seq, N_KV, HEAD_DIM)[:, -WINDOW:])
            outs_a["ks"].append(k_all[:, -WINDOW:])
            outs_a["vs"].append(v_all[:, -WINDOW:])
            w_o = a_w_o[j]
        else:
            n_qkv = qc + 2 * kvc
            n_rest = b_w_qkv.shape[2] - n_qkv
            rest_w = IDX_HEADS * IDX_DIM + LANES
            proj = matmul(xn, b_w_qkv[j][:, :n_qkv].astype(BF16), name="qkv_proj")
            w_rest = jnp.pad(b_w_qkv[j][:, n_qkv:], ((0, 0), (0, rest_w - n_rest))).astype(BF16)
            rest = matmul(xn, w_rest, tn_target=384, name="idx_proj")
            q_b, k_f, k_b, v_f, v_b = qkv_post(proj, b_q_norm[j], b_k_norm[j], head_tables, n_heads,
                                               HEAD_DIM ** -0.5 * LOG2E)
            qi_b, tail, kidx_b = idx_post(rest, b_kidx_norm[j], idx_tables)
            kidx_f = tail[:, :IDX_DIM]
            w_idx = tail[:, IDX_DIM:IDX_DIM + IDX_HEADS]

            blk_q = jnp.arange(nqb, dtype=I32)
            nkt_p = jnp.tile(((blk_q + 1) * QBLOCK + KEY_TILE - 1) // KEY_TILE, nb).astype(I32)
            nat_p = jnp.tile(((blk_q + 1) * QBLOCK + ATT_TILE - 1) // ATT_TILE, nb).astype(I32)
            qrow = blk_q[:, None] * QBLOCK + jnp.arange(QBLOCK, dtype=I32)[None, :]
            lim_p = jnp.tile(((qrow // CHUNK + 1) * CHUNK)[:, None, :], (nb, 1, 1)).astype(I32)
            qi_p = _query_blocks(qi_b[:tp], nb * nqb, IDX_DIM)
            w_p = _query_blocks(w_idx[:tp], nb * nqb, 1).reshape(nb * nqb, 1, IDX_HEADS * QBLOCK)
            bias_p = dsa_select(nkt_p, qi_p, w_p, lim_p, kidx_b, nb, seq, min(TOPK_MAX, seq // 4))
            o_p = dsa_attend(nat_p, q_b, k_b, v_b, bias_p, nb, n_heads)

            n_keys = past + ds
            lk = (n_keys + 2 * ATT_TILE - 1) // (2 * ATT_TILE) * (2 * ATT_TILE)
            pad_k = lambda new, cached: jnp.pad(
                jnp.concatenate([cached.reshape(db, past, -1), new.reshape(db, ds, -1)], axis=1),
                ((0, 0), (0, lk - n_keys), (0, 0))).reshape(db * lk, -1)
            cached_rows = lambda c: cache_rows(c.reshape(db * past, N_KV, HEAD_DIM))
            pad_q = lambda x: jnp.pad(x.reshape(db, ds, -1), ((0, 0), (0, QBLOCK - ds), (0, 0))).reshape(
                db * QBLOCK, -1)
            nkt_s = jnp.full((db,), (n_keys + KEY_TILE - 1) // KEY_TILE, I32)
            nat_s = jnp.full((db,), (n_keys + ATT_TILE - 1) // ATT_TILE, I32)
            lim_s = jnp.tile(jnp.where(jnp.arange(QBLOCK) < ds, n_keys, 0).astype(I32)[None, None, :], (db, 1, 1))
            qi_s = _query_blocks(pad_q(qi_b[tp:]), db, IDX_DIM)
            w_s = _query_blocks(pad_q(w_idx[tp:]), db, 1).reshape(db, 1, IDX_HEADS * QBLOCK)
            bias_s = dsa_select(nkt_s, qi_s, w_s, lim_s, pad_k(kidx_b[tp:], cache_b_kidx[j].astype(BF16)), db, lk,
                                min(TOPK_MAX, n_keys // 4))
            o_s = dsa_attend(nat_s, pad_q(q_b[tp:]), pad_k(k_b[tp:], cached_rows(cache_b_k[j])),
                             pad_k(v_b[tp:], cached_rows(cache_b_v[j])), bias_s, db, n_heads)
            o_s = o_s.reshape(db, QBLOCK, qc)[:, :ds].reshape(ts, qc)

            outs_b["kp"].append(k_f[:tp].reshape(nb, seq, N_KV, HEAD_DIM))
            outs_b["vp"].append(v_f[:tp].reshape(nb, seq, N_KV, HEAD_DIM))
            outs_b["ip"].append(kidx_f[:tp].reshape(nb, seq, IDX_DIM))
            outs_b["ks"].append(k_f[tp:].reshape(db, ds, N_KV, HEAD_DIM))
            outs_b["vs"].append(v_f[tp:].reshape(db, ds, N_KV, HEAD_DIM))
            outs_b["is"].append(kidx_f[tp:].reshape(db, ds, IDX_DIM))
            w_o = b_w_o[j]
        h = matmul(jnp.concatenate([o_p, o_s], axis=0), w_o.astype(BF16), res=h, name="out_proj")
        h = moe_layer(h, ffn_norm[i], moe_w_group[i], moe_b_group[i], moe_w_expert[i], moe_b_expert[i],
                      moe_w_gate, moe_w_up, moe_w_down, i, split_rows=tp if i == depth - 1 else None)

    return (h[0].reshape(nb, seq, d), h[1].reshape(db, ds, d),
            jnp.stack(outs_a["kp"]), jnp.stack(outs_a["vp"]),
            jnp.stack(outs_b["kp"]), jnp.stack(outs_b["vp"]), jnp.stack(outs_b["ip"]),
            jnp.stack(outs_a["ks"]), jnp.stack(outs_a["vs"]),
            jnp.stack(outs_b["ks"]), jnp.stack(outs_b["vs"]), jnp.stack(outs_b["is"]))
```

```python
import functools
import math

import jax
import jax.numpy as jnp
from jax import lax
from jax.experimental import pallas as pl
from jax.experimental.pallas import tpu as pltpu

F32 = jnp.float32
BF16 = jnp.bfloat16
I32 = jnp.int32

LANES = 128
CHUNK = 64
HEAD_DIM = 128
N_KV = 8
WINDOW = 128
ROT_DIM = HEAD_DIM // 4
ROPE_THETA = 500000.0
IDX_HEADS = 16
IDX_DIM = 64
IDX_ROT = IDX_DIM // 4
TOPK_MAX = 256
QBLOCK = 128
N_GROUPS = 8
EXP_PER_GROUP = 8
N_EXPERTS = N_GROUPS * EXP_PER_GROUP
RMS_EPS = 1e-6
NEG = -1e30
INT_MIN = -2 ** 31
KEY_TILE = 256
COUNT_ROWS = 1024
SEARCH_GROUP = 8
ATT_TILE = 512
LONG_TRIP = 4
HEADS_PER_STEP = 2
LOG2E = 1.4426950408889634
MOE_ROWS = 384
VMEM_LIMIT = 56 * 1024 * 1024

_NT = (((1,), (1,)), ((), ()))


def _cparams(*sem):
    return pltpu.CompilerParams(dimension_semantics=sem, vmem_limit_bytes=VMEM_LIMIT)


def _pick_tile(n, target, mult):
    best = None
    t = mult
    while t <= min(n, target):
        if n % t == 0:
            best = t
        t += mult
    return best if best is not None else n


def _rmsnorm_kernel(x_ref, g_ref, *o_refs):
    x = x_ref[...]
    y = x * lax.rsqrt(jnp.mean(x * x, axis=-1, keepdims=True) + RMS_EPS) * g_ref[...]
    for o_ref in o_refs:
        o_ref[...] = y.astype(o_ref.dtype)


def rmsnorm(x, g, out_dtypes):
    t, d = x.shape
    tr = _pick_tile(t, 256, 16)
    outs = pl.pallas_call(
        _rmsnorm_kernel,
        grid=(t // tr,),
        in_specs=[pl.BlockSpec((tr, d), lambda i: (i, 0)),
                  pl.BlockSpec((1, d), lambda i: (0, 0))],
        out_specs=[pl.BlockSpec((tr, d), lambda i: (i, 0)) for _ in out_dtypes],
        out_shape=[jax.ShapeDtypeStruct((t, d), dt) for dt in out_dtypes],
        compiler_params=_cparams("parallel"),
        name="rmsnorm",
    )(x, g.reshape(1, d).astype(F32))
    return outs


def _rmsnorm_pack_kernel(x_ref, g_ref, b_ref, p_ref):
    x = x_ref[...]
    y = (x * lax.rsqrt(jnp.mean(x * x, axis=-1, keepdims=True) + RMS_EPS) * g_ref[...]).astype(BF16)
    b_ref[...] = y
    half = y.shape[1] // 2
    hi = pltpu.bitcast(y[:, :half].astype(F32), I32)
    lo = pltpu.bitcast(y[:, half:].astype(F32), I32)
    p_ref[...] = hi | lax.shift_right_logical(lo, 16)


def rmsnorm_pack(x, g):
    t, d = x.shape
    tr = _pick_tile(t, 256, 16)
    return pl.pallas_call(
        _rmsnorm_pack_kernel,
        grid=(t // tr,),
        in_specs=[pl.BlockSpec((tr, d), lambda i: (i, 0)),
                  pl.BlockSpec((1, d), lambda i: (0, 0))],
        out_specs=[pl.BlockSpec((tr, d), lambda i: (i, 0)), pl.BlockSpec((tr, d // 2), lambda i: (i, 0))],
        out_shape=[jax.ShapeDtypeStruct((t, d), BF16), jax.ShapeDtypeStruct((t, d // 2), I32)],
        compiler_params=_cparams("parallel"),
        name="rmsnorm_pack",
    )(x, g.reshape(1, d).astype(F32))


def _unpack_rows(p):
    first = pltpu.bitcast(p & jnp.int32(-65536), F32).astype(BF16)
    second = pltpu.bitcast(lax.shift_left(p, 16), F32).astype(BF16)
    return first, second


def _mm_kernel(a_ref, b_ref, o_ref):
    o_ref[...] = jnp.dot(a_ref[...], b_ref[...], preferred_element_type=F32).astype(o_ref.dtype)


def _mm_res_kernel(a_ref, b_ref, r_ref, o_ref):
    o_ref[...] = r_ref[...] + jnp.dot(a_ref[...], b_ref[...], preferred_element_type=F32)


def matmul(a, b, res=None, tm_target=1280, tn_target=512, name="matmul"):
    m, k = a.shape
    _, n = b.shape
    tm = _pick_tile(m, tm_target, 16)
    tn = _pick_tile(n, tn_target, LANES)
    in_specs = [pl.BlockSpec((tm, k), lambda i, j: (i, 0)),
                pl.BlockSpec((k, tn), lambda i, j: (0, j))]
    args = [a, b]
    if res is not None:
        in_specs.append(pl.BlockSpec((tm, tn), lambda i, j: (i, j)))
        args.append(res)
    return pl.pallas_call(
        _mm_kernel if res is None else _mm_res_kernel,
        grid=(m // tm, n // tn),
        in_specs=in_specs,
        out_specs=pl.BlockSpec((tm, tn), lambda i, j: (i, j)),
        out_shape=jax.ShapeDtypeStruct((m, n), F32),
        compiler_params=_cparams("parallel", "arbitrary"),
        name=name,
    )(*args)


def rope_tables(pos, rot_dim, period):
    half = rot_dim // 2
    inv_freq = 1.0 / (ROPE_THETA ** (jnp.arange(half, dtype=F32) * (2.0 / rot_dim)))
    ang = pos.astype(F32)[:, None] * inv_freq[None, :]
    cos, sin = jnp.cos(ang), jnp.sin(ang)
    t = pos.shape[0]
    rest = period - rot_dim
    c = jnp.concatenate([cos, cos, jnp.ones((t, rest), F32)], axis=1)
    s1 = jnp.concatenate([-sin, jnp.zeros((t, half + rest), F32)], axis=1)
    s2 = jnp.concatenate([jnp.zeros((t, half), F32), sin, jnp.zeros((t, rest), F32)], axis=1)
    rep = LANES // period
    return tuple(jnp.tile(x, (1, rep)) for x in (c, s1, s2))


def _rope(y, c, s1, s2, half):
    return y * c + pltpu.roll(y, LANES - half, 1) * s1 + pltpu.roll(y, half, 1) * s2


def _qkv_post_kernel(p_ref, qg_ref, kg_ref, c_ref, s1_ref, s2_ref,
                     q_ref, kf_ref, kb_ref, vf_ref, vb_ref, *, n_heads, q_scale):
    c, s1, s2 = c_ref[...], s1_ref[...], s2_ref[...]

    def normrope(x, g):
        y = x * lax.rsqrt(jnp.mean(x * x, axis=-1, keepdims=True) + RMS_EPS) * g
        return _rope(y, c, s1, s2, ROT_DIM // 2)

    qg, kg = qg_ref[...], kg_ref[...]
    for h in range(n_heads):
        sl = slice(h * HEAD_DIM, (h + 1) * HEAD_DIM)
        q_ref[:, sl] = (normrope(p_ref[:, sl], qg) * q_scale).astype(BF16)
    for h in range(N_KV):
        sl = slice(h * HEAD_DIM, (h + 1) * HEAD_DIM)
        k = normrope(p_ref[:, (n_heads + h) * HEAD_DIM:(n_heads + h + 1) * HEAD_DIM], kg)
        kf_ref[:, h, :] = k
        kb_ref[:, sl] = k.astype(BF16)
        v = p_ref[:, (n_heads + N_KV + h) * HEAD_DIM:(n_heads + N_KV + h + 1) * HEAD_DIM]
        vf_ref[:, h, :] = v
        vb_ref[:, sl] = v.astype(BF16)


def qkv_post(proj, q_gain, k_gain, tables, n_heads, q_scale):
    t, ncol = proj.shape
    tr = _pick_tile(t, 256, 16)
    kvc = N_KV * HEAD_DIM
    qc = n_heads * HEAD_DIM
    row = lambda w: pl.BlockSpec((tr, w), lambda i: (i, 0))
    row3 = pl.BlockSpec((tr, N_KV, HEAD_DIM), lambda i: (i, 0, 0))
    vec = pl.BlockSpec((1, HEAD_DIM), lambda i: (0, 0))
    cache = jax.ShapeDtypeStruct((t, N_KV, HEAD_DIM), F32)
    return pl.pallas_call(
        functools.partial(_qkv_post_kernel, n_heads=n_heads, q_scale=q_scale),
        grid=(t // tr,),
        in_specs=[row(ncol), vec, vec, row(LANES), row(LANES), row(LANES)],
        out_specs=[row(qc), row3, row(kvc), row3, row(kvc)],
        out_shape=[jax.ShapeDtypeStruct((t, qc), BF16),
                   cache, jax.ShapeDtypeStruct((t, kvc), BF16),
                   cache, jax.ShapeDtypeStruct((t, kvc), BF16)],
        compiler_params=_cparams("parallel"),
        name="qkv_post",
    )(proj, q_gain.reshape(1, HEAD_DIM).astype(F32), k_gain.reshape(1, HEAD_DIM).astype(F32), *tables)


def qkv_project(xn, w, q_gain, k_gain, tables, n_heads, q_scale):
    n_qkv = (n_heads + 2 * N_KV) * HEAD_DIM
    proj = matmul(xn, w[:, :n_qkv].astype(BF16), name="qkv_proj")
    return qkv_post(proj, q_gain, k_gain, tables, n_heads, q_scale)


SAMPLE_KEY_ROWS = 256


def _sample_keys_kernel(c_ref, n_ref, o_ref, *, n_cache_tiles, n_new):
    t = pl.program_id(1)

    @pl.when(t < n_cache_tiles)
    def _():
        for h in range(N_KV):
            o_ref[:, h * HEAD_DIM:(h + 1) * HEAD_DIM] = c_ref[:, h, :].astype(BF16)

    @pl.when(t == n_cache_tiles)
    def _():
        o_ref[:n_new, :] = n_ref[...]
        o_ref[n_new:, :] = jnp.zeros((o_ref.shape[0] - n_new, o_ref.shape[1]), BF16)

    @pl.when(t > n_cache_tiles)
    def _():
        o_ref[...] = jnp.zeros(o_ref.shape, BF16)


def sample_keys(cache, new, row0, n_new, lk):
    nb, past = cache.shape[0], cache.shape[1]
    tr = SAMPLE_KEY_ROWS
    assert past % tr == 0 and lk % tr == 0 and lk > past and n_new < tr and n_new % 16 == 0 and row0 % n_new == 0
    n_cache_tiles = past // tr
    width = N_KV * HEAD_DIM
    return pl.pallas_call(
        functools.partial(_sample_keys_kernel, n_cache_tiles=n_cache_tiles, n_new=n_new),
        grid=(nb, lk // tr),
        in_specs=[pl.BlockSpec((None, tr, N_KV, HEAD_DIM),
                               lambda b, t: (b, jnp.minimum(t, n_cache_tiles - 1), 0, 0)),
                  pl.BlockSpec((n_new, width), lambda b, t: (row0 // n_new + b, 0))],
        out_specs=pl.BlockSpec((tr, width), lambda b, t: (b * (lk // tr) + t, 0)),
        out_shape=jax.ShapeDtypeStruct((nb * lk, width), BF16),
        compiler_params=_cparams("parallel", "arbitrary"),
        name="sample_keys",
    )(cache, new)


IDX_W_SCALE = (IDX_HEADS ** -0.5) * (IDX_DIM ** -0.5)


def _idx_post_kernel(r_ref, g_ref, c_ref, s1_ref, s2_ref, qi_ref, tail_ref, kb_ref):
    c, s1, s2 = c_ref[...], s1_ref[...], s2_ref[...]
    nq = IDX_HEADS * IDX_DIM
    for j in range(nq // LANES):
        sl = slice(j * LANES, (j + 1) * LANES)
        qi_ref[:, sl] = _rope(r_ref[:, sl], c, s1, s2, IDX_ROT // 2).astype(BF16)
    t = r_ref[:, nq:nq + LANES]
    lane = lax.broadcasted_iota(I32, t.shape, 1)
    is_k = lane < IDX_DIM
    ms = jnp.sum(jnp.where(is_k, t * t, 0.0), axis=-1, keepdims=True) * (1.0 / IDX_DIM)
    kn = t * lax.rsqrt(ms + RMS_EPS) * g_ref[...]
    tail = jnp.where(is_k, _rope(kn, c, s1, s2, IDX_ROT // 2), t * IDX_W_SCALE)
    tail_ref[...] = tail
    kb_ref[...] = tail[:, :IDX_DIM].astype(BF16)


def idx_post(rest, kidx_gain, tables):
    t, ncol = rest.shape
    tr = _pick_tile(t, 256, 16)
    nq = IDX_HEADS * IDX_DIM
    row = lambda w: pl.BlockSpec((tr, w), lambda i: (i, 0))
    g = jnp.concatenate([kidx_gain.astype(F32), jnp.zeros((LANES - IDX_DIM,), F32)]).reshape(1, LANES)
    return pl.pallas_call(
        _idx_post_kernel,
        grid=(t // tr,),
        in_specs=[row(ncol), pl.BlockSpec((1, LANES), lambda i: (0, 0)), row(LANES), row(LANES), row(LANES)],
        out_specs=[row(nq), row(LANES), row(IDX_DIM)],
        out_shape=[jax.ShapeDtypeStruct((t, nq), BF16), jax.ShapeDtypeStruct((t, LANES), F32),
                   jax.ShapeDtypeStruct((t, IDX_DIM), BF16)],
        compiler_params=_cparams("parallel"),
        name="idx_post",
    )(rest, g, *tables)


def _sink_attn(q, k, v, bias, sink):
    s = lax.dot_general(q, k, _NT, preferred_element_type=F32)
    if bias is not None:
        s = s + bias
    m = jnp.maximum(jnp.max(s, axis=-1, keepdims=True), sink)
    p = jnp.exp(s - m)
    denom = jnp.sum(p, axis=-1, keepdims=True) + jnp.exp(sink - m)
    o = jnp.dot(p.astype(BF16), v, preferred_element_type=F32)
    return o / denom


def _swa_prompt_kernel(sink_ref, q_ref, kp_ref, kc_ref, vp_ref, vc_ref, o_ref, *, group):
    i = pl.program_id(1)
    rows, cols = QBLOCK, 2 * QBLOCK
    qc = lax.broadcasted_iota(I32, (rows, cols), 0) // CHUNK
    kc = lax.broadcasted_iota(I32, (rows, cols), 1) // CHUNK
    first_valid = jnp.where(i > 0, 0, QBLOCK // CHUNK)
    vis = (kc >= qc) & (kc <= qc + WINDOW // CHUNK) & (kc >= first_valid)
    bias = jnp.where(vis, 0.0, NEG)
    for kv in range(N_KV):
        sl = slice(kv * HEAD_DIM, (kv + 1) * HEAD_DIM)
        k = jnp.concatenate([kp_ref[:, sl], kc_ref[:, sl]], axis=0)
        v = jnp.concatenate([vp_ref[:, sl], vc_ref[:, sl]], axis=0)
        for g in range(group):
            h = kv * group + g
            hs = slice(h * HEAD_DIM, (h + 1) * HEAD_DIM)
            o_ref[:, hs] = _sink_attn(q_ref[:, hs], k, v, bias, sink_ref[h]).astype(BF16)


def swa_prompt(q, k, v, sinks, n_batch, seq, n_heads):
    nqb = seq // QBLOCK
    qc = n_heads * HEAD_DIM
    kvc = N_KV * HEAD_DIM
    cur = lambda w: pl.BlockSpec((QBLOCK, w), lambda b, i: (b * nqb + i, 0))
    prev = lambda w: pl.BlockSpec((QBLOCK, w), lambda b, i: (b * nqb + jnp.maximum(i - 1, 0), 0))
    return pl.pallas_call(
        functools.partial(_swa_prompt_kernel, group=n_heads // N_KV),
        grid=(n_batch, nqb),
        in_specs=[pl.BlockSpec(memory_space=pltpu.SMEM),
                  cur(qc), prev(kvc), cur(kvc), prev(kvc), cur(kvc)],
        out_specs=cur(qc),
        out_shape=jax.ShapeDtypeStruct((n_batch * seq, qc), BF16),
        compiler_params=_cparams("parallel", "arbitrary"),
        name="swa_prompt",
    )(sinks.astype(F32), q, k, k, v, v)


def _swa_sample_kernel(sink_ref, q_ref, k_ref, v_ref, o_ref, *, group):
    for kv in range(N_KV):
        sl = slice(kv * HEAD_DIM, (kv + 1) * HEAD_DIM)
        k, v = k_ref[:, sl], v_ref[:, sl]
        for g in range(group):
            h = kv * group + g
            hs = slice(h * HEAD_DIM, (h + 1) * HEAD_DIM)
            o_ref[:, hs] = _sink_attn(q_ref[:, hs], k, v, None, sink_ref[h]).astype(BF16)


def swa_sample(q, k_all, v_all, sinks, row0, n_batch, n_new, n_heads):
    qc = n_heads * HEAD_DIM
    nk = k_all.shape[1]
    kvc = N_KV * HEAD_DIM
    return pl.pallas_call(
        functools.partial(_swa_sample_kernel, group=n_heads // N_KV),
        grid=(n_batch,),
        in_specs=[pl.BlockSpec(memory_space=pltpu.SMEM),
                  pl.BlockSpec((n_new, qc), lambda b: (row0 // n_new + b, 0)),
                  pl.BlockSpec((None, nk, kvc), lambda b: (b, 0, 0)),
                  pl.BlockSpec((None, nk, kvc), lambda b: (b, 0, 0))],
        out_specs=pl.BlockSpec((n_new, qc), lambda b: (b, 0)),
        out_shape=jax.ShapeDtypeStruct((n_batch * n_new, qc), BF16),
        compiler_params=_cparams("parallel"),
        name="swa_sample",
    )(sinks.astype(F32), q, k_all, v_all)


def _select_kernel(nkt_ref, qi_ref, w_ref, lim_ref, kidx_ref, bias_ref, keys_ref, p_ref,
                   *, topk, n_tiles, idx_bits):
    tk = KEY_TILE
    blk = pl.program_id(0) * pl.num_programs(1) + pl.program_id(1)
    nkt = nkt_ref[blk]
    qi = qi_ref[...]
    w = w_ref[...]
    lim = lim_ref[...]
    imin = jnp.int32(INT_MIN)

    def tile_rows(kt):
        return pl.ds(pl.multiple_of(kt * tk, tk), tk)

    def key_index(kt):
        return kt * tk + lax.broadcasted_iota(I32, (tk, LANES), 0)

    def score_tile(kt, carry):
        d = lax.dot_general(kidx_ref[tile_rows(kt), :], qi, _NT, preferred_element_type=F32)
        r = jnp.maximum(d, 0.0) * w
        acc = r[:, 0:LANES]
        for h in range(1, IDX_HEADS):
            acc = acc + r[:, h * LANES:(h + 1) * LANES]
        bits = pltpu.bitcast(acc, I32)
        key = bits ^ ((bits >> 31) & jnp.int32(0x7FFFFFFF))
        keys_ref[tile_rows(kt), :] = jnp.where(key_index(kt) < lim, key, imin)
        return carry

    lax.fori_loop(0, nkt, score_tile, 0)

    per_trip = COUNT_ROWS // tk
    n_trips = (nkt + per_trip - 1) // per_trip

    def inadmissible_tile(kt, carry):
        keys_ref[tile_rows(kt), :] = jnp.full((tk, LANES), imin, I32)
        return carry

    lax.fori_loop(nkt, n_trips * per_trip, inadmissible_tile, 0)

    def count(pred):
        def body(j, acc):
            rows = pl.ds(pl.multiple_of(j * COUNT_ROWS, COUNT_ROWS), COUNT_ROWS)
            index = j * COUNT_ROWS + lax.broadcasted_iota(I32, (COUNT_ROWS, LANES), 0)
            c = jnp.where(pred(keys_ref[rows, :], index), 1, 0).astype(I32)
            parts = [c[r * 8:(r + 1) * 8, :] for r in range(COUNT_ROWS // 8)]
            while len(parts) > 1:
                parts = [parts[a] + parts[a + 1] for a in range(0, len(parts), 2)]
            return acc + parts[0]
        acc = lax.fori_loop(0, n_trips, body, jnp.zeros((8, LANES), I32))
        return jnp.sum(acc, axis=0, keepdims=True)

    def search_bit(i, state):
        v, settled = state
        cand = v + lax.shift_left(jnp.int32(1), 31 - i)
        cnt = count(lambda kk, _: kk >= cand)
        v = jnp.where((settled == 0) & (cnt >= topk), cand, v)
        return v, jnp.where(cnt == topk, 1, settled)

    def search_group(state):
        g, v, settled, _ = state
        v, settled = lax.fori_loop(g * SEARCH_GROUP, (g + 1) * SEARCH_GROUP, search_bit, (v, settled))
        return g + 1, v, settled, jnp.min(settled)

    settled0 = jnp.where(lim < topk, 1, 0).astype(I32)
    _, tau, _, _ = lax.while_loop(lambda s: (s[0] < 32 // SEARCH_GROUP) & (s[3] == 0), search_group,
                                  (jnp.int32(0), jnp.full((1, LANES), imin, I32), settled0, jnp.min(settled0)))

    n_ge = count(lambda kk, _: (kk >= tau) & (kk != imin))
    n_gt = count(lambda kk, _: kk > tau)
    need = topk - n_gt

    p_ref[...] = jnp.full((1, LANES), 2 ** 30, I32)

    @pl.when(jnp.max(n_ge) > topk)
    def _():
        def index_bit(i, p):
            cand = p + lax.shift_left(jnp.int32(1), idx_bits - 1 - i)
            cnt = count(lambda kk, si: (kk == tau) & (si < cand))
            return jnp.where(cnt < need, cand, p)
        p_ref[...] = lax.fori_loop(0, idx_bits, index_bit, jnp.zeros((1, LANES), I32))

    p_last = p_ref[...]

    def write_tile(kt, carry):
        kk = keys_ref[tile_rows(kt), :]
        sel = (kk > tau) | ((kk == tau) & (key_index(kt) <= p_last) & (kk != imin))
        bias_ref[tile_rows(kt), :] = jnp.where(sel, 0.0, NEG).astype(BF16)
        return carry

    lax.fori_loop(0, nkt, write_tile, 0)

    def fill_tile(kt, carry):
        bias_ref[tile_rows(kt), :] = jnp.full((tk, LANES), NEG, BF16)
        return carry

    lax.fori_loop(nkt, n_tiles, fill_tile, 0)


def dsa_select(nkt, qi_blocks, w_blocks, limits, kidx, nb, lk, topk):
    nq = qi_blocks.shape[0] // nb
    assert lk % COUNT_ROWS == 0, "the counting passes walk whole COUNT_ROWS-row trips"
    n_tiles = lk // KEY_TILE
    idx_bits = max(1, int(lk).bit_length())
    per_block = lambda shape: pl.BlockSpec((None,) + shape, lambda b, q, *_: (b * nq + q, 0, 0))
    grid_spec = pltpu.PrefetchScalarGridSpec(
        num_scalar_prefetch=1,
        grid=(nb, nq),
        in_specs=[per_block((IDX_HEADS * QBLOCK, IDX_DIM)),
                  per_block((1, IDX_HEADS * QBLOCK)),
                  per_block((1, LANES)),
                  pl.BlockSpec((lk, IDX_DIM), lambda b, q, *_: (b, 0))],
        out_specs=per_block((lk, LANES)),
        scratch_shapes=[pltpu.VMEM((lk, LANES), I32), pltpu.VMEM((1, LANES), I32)],
    )
    return pl.pallas_call(
        functools.partial(_select_kernel, topk=topk, n_tiles=n_tiles, idx_bits=idx_bits),
        grid_spec=grid_spec,
        out_shape=jax.ShapeDtypeStruct((nb * nq, lk, LANES), BF16),
        compiler_params=_cparams("parallel", "arbitrary"),
        name="dsa_select",
    )(nkt, qi_blocks, w_blocks, limits, kidx)


def _masked_attn_kernel(nkt_ref, q_ref, k_ref, v_ref, bias_ref, o_ref, m_ref, acc_ref, s0_ref, s1_ref,
                        *, group, n_tiles):
    tk = ATT_TILE
    blk = pl.program_id(0) * pl.num_programs(2) + pl.program_id(2)
    nkt = nkt_ref[blk]
    rows = group * QBLOCK
    gw = group * HEAD_DIM
    heads = range(HEADS_PER_STEP)
    eye = jnp.where(lax.broadcasted_iota(I32, (QBLOCK, QBLOCK), 0)
                    == lax.broadcasted_iota(I32, (QBLOCK, QBLOCK), 1), 1.0, 0.0).astype(BF16)
    q_aug = [jnp.concatenate(
        [jnp.concatenate([q_ref[:, hh * gw + g * HEAD_DIM:hh * gw + (g + 1) * HEAD_DIM], eye], axis=1)
         for g in range(group)], axis=0) for hh in heads]
    ones = jnp.ones((tk, HEAD_DIM), BF16)
    m_ref[...] = jnp.full(m_ref.shape, NEG, F32)
    acc_ref[...] = jnp.zeros(acc_ref.shape, F32)

    def tile_rows(kt):
        return pl.ds(pl.multiple_of(jnp.minimum(kt, n_tiles - 1) * tk, tk), tk)

    def head_cols(hh):
        return slice(hh * HEAD_DIM, (hh + 1) * HEAD_DIM)

    def scores(s_ref, kt):
        sl = tile_rows(kt)
        bias = bias_ref[sl, :]
        for hh in heads:
            k_aug = jnp.concatenate([k_ref[sl, head_cols(hh)], bias], axis=1)
            s_ref[hh] = lax.dot_general(q_aug[hh], k_aug, _NT, preferred_element_type=F32)

    def update(s_ref, kt):
        sl = tile_rows(kt)
        for hh in heads:
            s = s_ref[hh]
            m_prev = m_ref[hh]
            m_new = jnp.maximum(m_prev, jnp.max(s, axis=-1, keepdims=True))
            alpha = jnp.exp2(m_prev - m_new)
            p = jnp.exp2(s - jnp.concatenate([m_new] * (tk // LANES), axis=1))
            v_aug = jnp.concatenate([v_ref[sl, head_cols(hh)], ones], axis=1)
            pv = jnp.dot(p.astype(BF16), v_aug, preferred_element_type=F32)
            acc_ref[hh] = jnp.concatenate([alpha, alpha], axis=1) * acc_ref[hh] + pv
            m_ref[hh] = m_new

    scores(s0_ref, 0)

    def run(first, n_trips, per_trip):
        def body(u, carry):
            base = first + per_trip * u
            for i in range(0, per_trip, 2):
                scores(s1_ref, base + i + 1)
                update(s0_ref, base + i)
                scores(s0_ref, base + i + 2)
                update(s1_ref, base + i + 1)
            return carry
        lax.fori_loop(0, n_trips, body, 0)

    n_long = nkt // LONG_TRIP
    run(0, n_long, LONG_TRIP)
    run(n_long * LONG_TRIP, (nkt - n_long * LONG_TRIP + 1) // 2, 2)
    for hh in heads:
        acc = acc_ref[hh]
        o = acc[:, :HEAD_DIM] / jnp.maximum(acc[:, HEAD_DIM:], 1e-30)
        for g in range(group):
            o_ref[:, hh * gw + g * HEAD_DIM:hh * gw + (g + 1) * HEAD_DIM] = (
                o[g * QBLOCK:(g + 1) * QBLOCK].astype(BF16))


def dsa_attend(nkt, q, k, v, bias, nb, n_heads):
    lk = bias.shape[1]
    nq = bias.shape[0] // nb
    group = n_heads // N_KV
    gw = group * HEAD_DIM
    rows = group * QBLOCK
    n_tiles = lk // ATT_TILE
    assert lk % (2 * ATT_TILE) == 0, "the tile-pair loop needs an even number of key tiles"
    hps = HEADS_PER_STEP
    qspec = pl.BlockSpec((QBLOCK, hps * gw), lambda b, h, q, *_: (b * nq + q, h))
    kvspec = pl.BlockSpec((lk, hps * HEAD_DIM), lambda b, h, q, *_: (b, h))
    grid_spec = pltpu.PrefetchScalarGridSpec(
        num_scalar_prefetch=1,
        grid=(nb, N_KV // hps, nq),
        in_specs=[qspec, kvspec, kvspec,
                  pl.BlockSpec((None, lk, LANES), lambda b, h, q, *_: (b * nq + q, 0, 0))],
        out_specs=qspec,
        scratch_shapes=[pltpu.VMEM((hps, rows, LANES), F32),
                        pltpu.VMEM((hps, rows, 2 * HEAD_DIM), F32),
                        pltpu.VMEM((hps, rows, ATT_TILE), F32),
                        pltpu.VMEM((hps, rows, ATT_TILE), F32)],
    )
    return pl.pallas_call(
        functools.partial(_masked_attn_kernel, group=group, n_tiles=n_tiles),
        grid_spec=grid_spec,
        out_shape=jax.ShapeDtypeStruct((nb * nq * QBLOCK, q.shape[1]), BF16),
        compiler_params=_cparams("parallel", "parallel", "arbitrary"),
        name="dsa_attend",
    )(nkt, q, k, v, bias)


def _router_kernel(l_ref, b_ref, id_ref, gate_ref):
    x = l_ref[...] + b_ref[...]
    lane = lax.broadcasted_iota(I32, x.shape, 1)
    big = jnp.int32(2 ** 30)
    ninf = -jnp.inf

    def first_max(vals):
        m = jnp.max(vals, axis=-1, keepdims=True)
        return m, jnp.min(jnp.where(vals == m, lane, big), axis=-1, keepdims=True)

    glog = jnp.where(lane < N_GROUPS, x, ninf)
    gmax, gsel = first_max(glog)
    gprob = 1.0 / jnp.sum(jnp.exp(glog - gmax), axis=-1, keepdims=True)
    in_group = (lane >= N_GROUPS) & ((lane - N_GROUPS) // EXP_PER_GROUP == gsel) & (lane < N_GROUPS + N_EXPERTS)
    e1 = jnp.where(in_group, x, ninf)
    v1, i1 = first_max(e1)
    v2, i2 = first_max(jnp.where(lane == i1, ninf, e1))
    t = jnp.exp(v2 - v1)
    g1 = gprob / (1.0 + t)
    g2 = gprob * t / (1.0 + t)
    id_ref[...] = jnp.where(lane == 0, i1 - N_GROUPS, jnp.where(lane == 1, i2 - N_GROUPS, 0))
    gate_ref[...] = jnp.where(lane == 0, g1, jnp.where(lane == 1, g2, 0.0))


def router(logits, bias):
    t = logits.shape[0]
    tr = _pick_tile(t, 256, 8)
    row = pl.BlockSpec((tr, LANES), lambda i: (i, 0))
    return pl.pallas_call(
        _router_kernel,
        grid=(t // tr,),
        in_specs=[row, pl.BlockSpec((1, LANES), lambda i: (0, 0))],
        out_specs=[row, row],
        out_shape=[jax.ShapeDtypeStruct((t, LANES), I32), jax.ShapeDtypeStruct((t, LANES), F32)],
        compiler_params=_cparams("parallel"),
        name="router",
    )(logits, bias)


DMA_UNROLL = 8


def _row_copy(src_hbm, row, dst_ref, r, sem):
    return pltpu.make_async_copy(src_hbm.at[pl.ds(row, 1), :], dst_ref.at[pl.ds(r, 1), :], sem)


def _start_row_gather(src_hbm, row_of, dst_ref, sem, n_rows, beside_block_loads=False):
    def trip(j, carry):
        for u in range(DMA_UNROLL):
            r = j * DMA_UNROLL + u
            _row_copy(src_hbm, row_of(r), dst_ref, r, sem).start(priority=1 if beside_block_loads else u % 2)
        return carry
    lax.fori_loop(0, n_rows // DMA_UNROLL, trip, 0)


def _wait_row_gather(src_hbm, dst_ref, sem, n_rows):
    pltpu.make_async_copy(src_hbm.at[pl.ds(0, n_rows), :], dst_ref, sem).wait()


def _experts_kernel(tok_ref, bexp_ref, nused_ref, x_hbm, wg_ref, wu_ref, wd_ref, o_ref, xbuf_ref, sem_ref):
    i, j = pl.program_id(0), pl.program_id(1)
    n_used = nused_ref[0]
    slot = i % 2
    first = j == 0

    def start_block(block, buf):
        base = block * MOE_ROWS
        _start_row_gather(x_hbm, lambda r: tok_ref[base + r], xbuf_ref.at[buf], sem_ref.at[buf], MOE_ROWS,
                          beside_block_loads=True)

    @pl.when(first & (i == 0))
    def _():
        start_block(0, 0)

    @pl.when(first & (i + 1 < n_used))
    def _():
        start_block(i + 1, 1 - slot)

    @pl.when(first & (i < n_used))
    def _():
        _wait_row_gather(x_hbm, xbuf_ref.at[slot], sem_ref.at[slot], MOE_ROWS)

    @pl.when(i < n_used)
    def _():
        x0, x1 = _unpack_rows(xbuf_ref[slot])
        half = x0.shape[1]

        def proj(w_ref):
            return (jnp.dot(x0, w_ref[:half, :].astype(BF16), preferred_element_type=F32)
                    + jnp.dot(x1, w_ref[half:, :].astype(BF16), preferred_element_type=F32))

        a = proj(wg_ref)
        b = proj(wu_ref)
        hid = (a * (1.0 / (1.0 + jnp.exp(-a)))) * b
        y = jnp.dot(hid.astype(BF16), wd_ref[...].astype(BF16), preferred_element_type=F32)

        @pl.when(first)
        def _():
            o_ref[...] = y

        @pl.when(jnp.logical_not(first))
        def _():
            o_ref[...] += y

    @pl.when(first & (i >= n_used))
    def _():
        o_ref[...] = jnp.zeros(o_ref.shape, o_ref.dtype)


def grouped_experts(tok_buf, blk_exp, n_used, x, w_gate, w_up, w_down, layer, n_blocks):
    d, f = w_gate.shape[2], w_gate.shape[3]
    fh = f // 2

    def expert_of(i, be, nu):
        return be[jnp.minimum(i, nu[0] - 1)]

    def half_of(i, j, nu):
        ii = jnp.minimum(i, nu[0] - 1)
        jj = jnp.where(i < nu[0], j, 1)
        return jnp.where(ii % 2 == 0, jj, 1 - jj)

    w_in = pl.BlockSpec((None, None, d, fh),
                        lambda i, j, tok, be, nu: (layer, expert_of(i, be, nu), 0, half_of(i, j, nu)))
    w_out = pl.BlockSpec((None, None, fh, d),
                         lambda i, j, tok, be, nu: (layer, expert_of(i, be, nu), half_of(i, j, nu), 0))
    grid_spec = pltpu.PrefetchScalarGridSpec(
        num_scalar_prefetch=3,
        grid=(n_blocks, 2),
        in_specs=[pl.BlockSpec(memory_space=pl.ANY), w_in, w_in, w_out],
        out_specs=pl.BlockSpec((MOE_ROWS, d), lambda i, j, *_: (i, 0)),
        scratch_shapes=[pltpu.VMEM((2, MOE_ROWS, x.shape[1]), x.dtype), pltpu.SemaphoreType.DMA((2,))],
    )
    return pl.pallas_call(
        _experts_kernel,
        grid_spec=grid_spec,
        out_shape=jax.ShapeDtypeStruct((n_blocks * MOE_ROWS, d), F32),
        compiler_params=_cparams("arbitrary", "arbitrary"),
        name="moe_experts",
    )(tok_buf, blk_exp, n_used, x, w_gate, w_up, w_down)


def _combine_kernel(p0_ref, p1_ref, h_ref, g_ref, y_hbm, *rest, rows, split_step):
    o_refs, (buf_ref, sem_ref) = rest[:-2], rest[-2:]
    i = pl.program_id(0)
    base = i * rows
    _start_row_gather(y_hbm, lambda r: p0_ref[base + r], buf_ref.at[0], sem_ref.at[0], rows)
    _start_row_gather(y_hbm, lambda r: p1_ref[base + r], buf_ref.at[1], sem_ref.at[1], rows)
    g = g_ref[...]
    g0, g1 = g[:, 0:1], g[:, 1:2]
    _wait_row_gather(y_hbm, buf_ref.at[0], sem_ref.at[0], rows)
    _wait_row_gather(y_hbm, buf_ref.at[1], sem_ref.at[1], rows)
    out = h_ref[...] + (g0 * buf_ref[0] + g1 * buf_ref[1])
    if split_step is None:
        o_refs[0][...] = out
    else:
        @pl.when(i < split_step)
        def _():
            o_refs[0][...] = out

        @pl.when(i >= split_step)
        def _():
            o_refs[1][...] = out


def combine(pos0, pos1, h, gates, y_sorted, split_rows=None):
    t, d = h.shape
    rows = _pick_tile(t if split_rows is None else math.gcd(split_rows, t - split_rows), 128, 8)
    row = lambda w: pl.BlockSpec((rows, w), lambda i, *_: (i, 0))
    if split_rows is None:
        split_step = None
        out_specs = row(d)
        out_shape = jax.ShapeDtypeStruct((t, d), F32)
    else:
        split_step = split_rows // rows
        out_specs = [pl.BlockSpec((rows, d), lambda i, *_: (jnp.minimum(i, split_step - 1), 0)),
                     pl.BlockSpec((rows, d), lambda i, *_: (jnp.maximum(i - split_step, 0), 0))]
        out_shape = [jax.ShapeDtypeStruct((split_rows, d), F32), jax.ShapeDtypeStruct((t - split_rows, d), F32)]
    grid_spec = pltpu.PrefetchScalarGridSpec(
        num_scalar_prefetch=2,
        grid=(t // rows,),
        in_specs=[row(d), row(LANES), pl.BlockSpec(memory_space=pl.ANY)],
        out_specs=out_specs,
        scratch_shapes=[pltpu.VMEM((2, rows, d), F32), pltpu.SemaphoreType.DMA((2,))],
    )
    return pl.pallas_call(
        functools.partial(_combine_kernel, rows=rows, split_step=split_step),
        grid_spec=grid_spec,
        out_shape=out_shape,
        compiler_params=_cparams("arbitrary"),
        name="moe_combine",
    )(pos0, pos1, h, gates, y_sorted)


def _dispatch_plan(expert, n_tokens):
    a = expert.shape[0]
    order = jnp.argsort(expert, stable=True).astype(I32)
    rank = jnp.argsort(order).astype(I32)
    experts = jnp.arange(N_EXPERTS, dtype=I32)
    counts = jnp.sum((expert[:, None] == experts[None, :]).astype(I32), axis=0)
    start = jnp.cumsum(counts) - counts
    pcounts = (counts + MOE_ROWS - 1) // MOE_ROWS * MOE_ROWS
    pend = jnp.cumsum(pcounts)
    pstart = pend - pcounts
    slot = pstart[expert] + (rank - start[expert])
    n_blocks = (a + MOE_ROWS - 1) // MOE_ROWS + N_EXPERTS
    block_row0 = jnp.arange(n_blocks, dtype=I32) * MOE_ROWS
    blk_exp = jnp.minimum(jnp.sum((pend[None, :] <= block_row0[:, None]).astype(I32), axis=1),
                          N_EXPERTS - 1).astype(I32)
    slot_exp = jnp.repeat(blk_exp, MOE_ROWS)
    within = jnp.arange(n_blocks * MOE_ROWS, dtype=I32) - pstart[slot_exp]
    src = order[jnp.clip(start[slot_exp] + within, 0, a - 1)]
    tok_buf = jnp.where((within >= 0) & (within < counts[slot_exp]), src // 2, 0).astype(I32)
    n_used = (pend[-1] // MOE_ROWS).astype(I32).reshape(1)
    return tok_buf, slot.astype(I32), blk_exp, n_used, n_blocks


def moe_layer(h, norm_g, w_group, b_group, w_expert, b_expert, w_gate, w_up, w_down, layer, split_rows=None):
    t, d = h.shape
    xn_b, xn_p = rmsnorm_pack(h, norm_g)
    n_logit = N_GROUPS + N_EXPERTS
    w_r = jnp.concatenate([w_group, w_expert, jnp.zeros((d, LANES - n_logit), F32)], axis=1).astype(BF16)
    b_r = jnp.concatenate([b_group.astype(F32), b_expert.astype(F32),
                           jnp.zeros((LANES - n_logit,), F32)]).reshape(1, LANES)
    ids, gates = router(matmul(xn_b, w_r, name="router_logits"), b_r)
    tok_buf, slot, blk_exp, n_used, n_blocks = _dispatch_plan(ids[:, :2].reshape(-1), t)
    y_sorted = grouped_experts(tok_buf, blk_exp, n_used, xn_p, w_gate, w_up, w_down, layer, n_blocks)
    slot2 = slot.reshape(t, 2)
    return combine(slot2[:, 0], slot2[:, 1], h, gates, y_sorted, split_rows)


def _query_blocks(x, n_blocks, width):
    return x.reshape(n_blocks, QBLOCK, IDX_HEADS, width).transpose(0, 2, 1, 3).reshape(
        n_blocks, IDX_HEADS * QBLOCK, width)


def kernel(x_prompt, x_sample, cache_a_k, cache_a_v, cache_b_k, cache_b_v, cache_b_kidx, attn_norm, ffn_norm, a_w_qkv, a_w_o, a_q_norm, a_k_norm, a_sinks, b_w_qkv, b_w_o, b_q_norm, b_k_norm, b_kidx_norm, moe_w_group, moe_b_group, moe_w_expert, moe_b_expert, moe_w_gate, moe_w_up, moe_w_down):
    nb, seq, d = x_prompt.shape
    db, ds, _ = x_sample.shape
    depth = attn_norm.shape[0]
    n_heads = d // HEAD_DIM
    qc = n_heads * HEAD_DIM
    kvc = N_KV * HEAD_DIM
    tp, ts = nb * seq, db * ds
    past = cache_b_k.shape[2]
    nqb = seq // QBLOCK

    h = jnp.concatenate([x_prompt.reshape(tp, d), x_sample.reshape(ts, d)], axis=0)
    pos = jnp.concatenate([jnp.tile(jnp.arange(seq), nb), jnp.tile(past + jnp.arange(ds), db)])
    head_tables = rope_tables(pos, ROT_DIM, HEAD_DIM)
    idx_tables = rope_tables(pos, IDX_ROT, IDX_DIM)

    outs_a = {"kp": [], "vp": [], "ks": [], "vs": []}
    outs_b = {"kp": [], "vp": [], "ip": [], "ks": [], "vs": [], "is": []}
    for i in range(depth):
        j = i // 2
        xn = rmsnorm(h, attn_norm[i], (BF16,))[0]
        if i % 2 == 0:
            q_b, k_f, k_b, v_f, v_b = qkv_project(xn, a_w_qkv[j], a_q_norm[j], a_k_norm[j], head_tables, n_heads,
                                                  HEAD_DIM ** -0.5)
            o_p = swa_prompt(q_b, k_b, v_b, a_sinks[j], nb, seq, n_heads)
            k_all = jnp.concatenate([cache_a_k[j].astype(F32), k_f[tp:].reshape(db, ds, N_KV, HEAD_DIM)], axis=1)
            v_all = jnp.concatenate([cache_a_v[j].astype(F32), v_f[tp:].reshape(db, ds, N_KV, HEAD_DIM)], axis=1)
            o_s = swa_sample(q_b, k_all.reshape(db, WINDOW + ds, kvc).astype(BF16),
                             v_all.reshape(db, WINDOW + ds, kvc).astype(BF16), a_sinks[j], tp, db, ds, n_heads)
            outs_a["kp"].append(k_f[:tp].reshape(nb, seq, N_KV, HEAD_DIM)[:, -WINDOW:])
            outs_a["vp"].append(v_f[:tp].reshape(nb, seq, N_KV, HEAD_DIM)[:, -WINDOW:])
            outs_a["ks"].append(k_all[:, -WINDOW:])
            outs_a["vs"].append(v_all[:, -WINDOW:])
            w_o = a_w_o[j]
        else:
            n_qkv = qc + 2 * kvc
            n_rest = b_w_qkv.shape[2] - n_qkv
            rest_w = IDX_HEADS * IDX_DIM + LANES
            w_rest = jnp.pad(b_w_qkv[j][:, n_qkv:], ((0, 0), (0, rest_w - n_rest))).astype(BF16)
            rest = matmul(xn, w_rest, tn_target=384, name="idx_proj")
            q_b, k_f, k_b, v_f, v_b = qkv_project(xn, b_w_qkv[j], b_q_norm[j], b_k_norm[j], head_tables, n_heads,
                                                  HEAD_DIM ** -0.5 * LOG2E)
            qi_b, tail, kidx_b = idx_post(rest, b_kidx_norm[j], idx_tables)
            kidx_f = tail[:, :IDX_DIM]
            w_idx = tail[:, IDX_DIM:IDX_DIM + IDX_HEADS]

            blk_q = jnp.arange(nqb, dtype=I32)
            nkt_p = jnp.tile(((blk_q + 1) * QBLOCK + KEY_TILE - 1) // KEY_TILE, nb).astype(I32)
            nat_p = jnp.tile(((blk_q + 1) * QBLOCK + ATT_TILE - 1) // ATT_TILE, nb).astype(I32)
            qrow = blk_q[:, None] * QBLOCK + jnp.arange(QBLOCK, dtype=I32)[None, :]
            lim_p = jnp.tile(((qrow // CHUNK + 1) * CHUNK)[:, None, :], (nb, 1, 1)).astype(I32)
            qi_p = _query_blocks(qi_b[:tp], nb * nqb, IDX_DIM)
            w_p = _query_blocks(w_idx[:tp], nb * nqb, 1).reshape(nb * nqb, 1, IDX_HEADS * QBLOCK)
            bias_p = dsa_select(nkt_p, qi_p, w_p, lim_p, kidx_b, nb, seq, min(TOPK_MAX, seq // 4))
            o_p = dsa_attend(nat_p, q_b, k_b, v_b, bias_p, nb, n_heads)

            n_keys = past + ds
            lk = (n_keys + 2 * ATT_TILE - 1) // (2 * ATT_TILE) * (2 * ATT_TILE)
            pad_k = lambda new, cached: jnp.pad(
                jnp.concatenate([cached.reshape(db, past, -1), new.reshape(db, ds, -1)], axis=1),
                ((0, 0), (0, lk - n_keys), (0, 0))).reshape(db * lk, -1)
            pad_q = lambda x: jnp.pad(x.reshape(db, ds, -1), ((0, 0), (0, QBLOCK - ds), (0, 0))).reshape(
                db * QBLOCK, -1)
            nkt_s = jnp.full((db,), (n_keys + KEY_TILE - 1) // KEY_TILE, I32)
            nat_s = jnp.full((db,), (n_keys + ATT_TILE - 1) // ATT_TILE, I32)
            lim_s = jnp.tile(jnp.where(jnp.arange(QBLOCK) < ds, n_keys, 0).astype(I32)[None, None, :], (db, 1, 1))
            qi_s = _query_blocks(pad_q(qi_b[tp:]), db, IDX_DIM)
            w_s = _query_blocks(pad_q(w_idx[tp:]), db, 1).reshape(db, 1, IDX_HEADS * QBLOCK)
            bias_s = dsa_select(nkt_s, qi_s, w_s, lim_s, pad_k(kidx_b[tp:], cache_b_kidx[j].astype(BF16)), db, lk,
                                min(TOPK_MAX, n_keys // 4))
            o_s = dsa_attend(nat_s, pad_q(q_b[tp:]), sample_keys(cache_b_k[j], k_b, tp, ds, lk),
                             sample_keys(cache_b_v[j], v_b, tp, ds, lk), bias_s, db, n_heads)
            o_s = o_s.reshape(db, QBLOCK, qc)[:, :ds].reshape(ts, qc)

            outs_b["kp"].append(k_f[:tp].reshape(nb, seq, N_KV, HEAD_DIM))
            outs_b["vp"].append(v_f[:tp].reshape(nb, seq, N_KV, HEAD_DIM))
            outs_b["ip"].append(kidx_f[:tp].reshape(nb, seq, IDX_DIM))
            outs_b["ks"].append(k_f[tp:].reshape(db, ds, N_KV, HEAD_DIM))
            outs_b["vs"].append(v_f[tp:].reshape(db, ds, N_KV, HEAD_DIM))
            outs_b["is"].append(kidx_f[tp:].reshape(db, ds, IDX_DIM))
            w_o = b_w_o[j]
        h = matmul(jnp.concatenate([o_p, o_s], axis=0), w_o.astype(BF16), res=h, name="out_proj")
        h = moe_layer(h, ffn_norm[i], moe_w_group[i], moe_b_group[i], moe_w_expert[i], moe_b_expert[i],
                      moe_w_gate, moe_w_up, moe_w_down, i, split_rows=tp if i == depth - 1 else None)

    return (h[0].reshape(nb, seq, d), h[1].reshape(db, ds, d),
            jnp.stack(outs_a["kp"]), jnp.stack(outs_a["vp"]),
            jnp.stack(outs_b["kp"]), jnp.stack(outs_b["vp"]), jnp.stack(outs_b["ip"]),
            jnp.stack(outs_a["ks"]), jnp.stack(outs_a["vs"]),
            jnp.stack(outs_b["ks"]), jnp.stack(outs_b["vs"]), jnp.stack(outs_b["is"]))
```

```python
import functools
import math

import jax
import jax.numpy as jnp
from jax import lax
from jax.experimental import pallas as pl
from jax.experimental.pallas import tpu as pltpu

F32 = jnp.float32
BF16 = jnp.bfloat16
I32 = jnp.int32

LANES = 128
CHUNK = 64
HEAD_DIM = 128
N_KV = 8
WINDOW = 128
ROT_DIM = HEAD_DIM // 4
ROPE_THETA = 500000.0
IDX_HEADS = 16
IDX_DIM = 64
IDX_ROT = IDX_DIM // 4
TOPK_MAX = 256
QBLOCK = 128
N_GROUPS = 8
EXP_PER_GROUP = 8
N_EXPERTS = N_GROUPS * EXP_PER_GROUP
RMS_EPS = 1e-6
NEG = -1e30
INT_MIN = -2 ** 31
KEY_TILE = 256
COUNT_ROWS = 512
SEARCH_GROUP = 4
ATT_TILE = 512
LONG_TRIP = 4
HEADS_PER_STEP = 2
LOG2E = 1.4426950408889634
MOE_ROWS = 256
VMEM_LIMIT = 56 * 1024 * 1024

_NT = (((1,), (1,)), ((), ()))


def _cparams(*sem):
    return pltpu.CompilerParams(dimension_semantics=sem, vmem_limit_bytes=VMEM_LIMIT)


def _pick_tile(n, target, mult):
    best = None
    t = mult
    while t <= min(n, target):
        if n % t == 0:
            best = t
        t += mult
    return best if best is not None else n


def _rmsnorm_kernel(x_ref, g_ref, *o_refs):
    x = x_ref[...]
    y = x * lax.rsqrt(jnp.mean(x * x, axis=-1, keepdims=True) + RMS_EPS) * g_ref[...]
    for o_ref in o_refs:
        o_ref[...] = y.astype(o_ref.dtype)


def rmsnorm(x, g, out_dtypes):
    t, d = x.shape
    tr = _pick_tile(t, 256, 16)
    outs = pl.pallas_call(
        _rmsnorm_kernel,
        grid=(t // tr,),
        in_specs=[pl.BlockSpec((tr, d), lambda i: (i, 0)),
                  pl.BlockSpec((1, d), lambda i: (0, 0))],
        out_specs=[pl.BlockSpec((tr, d), lambda i: (i, 0)) for _ in out_dtypes],
        out_shape=[jax.ShapeDtypeStruct((t, d), dt) for dt in out_dtypes],
        compiler_params=_cparams("parallel"),
        name="rmsnorm",
    )(x, g.reshape(1, d).astype(F32))
    return outs


def _rmsnorm_pack_kernel(x_ref, g_ref, b_ref, p_ref):
    x = x_ref[...]
    y = (x * lax.rsqrt(jnp.mean(x * x, axis=-1, keepdims=True) + RMS_EPS) * g_ref[...]).astype(BF16)
    b_ref[...] = y
    half = y.shape[1] // 2
    hi = pltpu.bitcast(y[:, :half].astype(F32), I32)
    lo = pltpu.bitcast(y[:, half:].astype(F32), I32)
    p_ref[...] = hi | lax.shift_right_logical(lo, 16)


def rmsnorm_pack(x, g):
    t, d = x.shape
    tr = _pick_tile(t, 256, 16)
    return pl.pallas_call(
        _rmsnorm_pack_kernel,
        grid=(t // tr,),
        in_specs=[pl.BlockSpec((tr, d), lambda i: (i, 0)),
                  pl.BlockSpec((1, d), lambda i: (0, 0))],
        out_specs=[pl.BlockSpec((tr, d), lambda i: (i, 0)), pl.BlockSpec((tr, d // 2), lambda i: (i, 0))],
        out_shape=[jax.ShapeDtypeStruct((t, d), BF16), jax.ShapeDtypeStruct((t, d // 2), I32)],
        compiler_params=_cparams("parallel"),
        name="rmsnorm_pack",
    )(x, g.reshape(1, d).astype(F32))


def _unpack_rows(p):
    first = pltpu.bitcast(p & jnp.int32(-65536), F32).astype(BF16)
    second = pltpu.bitcast(lax.shift_left(p, 16), F32).astype(BF16)
    return first, second


def _mm_kernel(a_ref, b_ref, o_ref):
    o_ref[...] = jnp.dot(a_ref[...], b_ref[...], preferred_element_type=F32).astype(o_ref.dtype)


def _mm_res_kernel(a_ref, b_ref, r_ref, o_ref):
    o_ref[...] = r_ref[...] + jnp.dot(a_ref[...], b_ref[...], preferred_element_type=F32)


def matmul(a, b, res=None, tm_target=1280, tn_target=512, name="matmul"):
    m, k = a.shape
    _, n = b.shape
    tm = _pick_tile(m, tm_target, 16)
    tn = _pick_tile(n, tn_target, LANES)
    in_specs = [pl.BlockSpec((tm, k), lambda i, j: (i, 0)),
                pl.BlockSpec((k, tn), lambda i, j: (0, j))]
    args = [a, b]
    if res is not None:
        in_specs.append(pl.BlockSpec((tm, tn), lambda i, j: (i, j)))
        args.append(res)
    return pl.pallas_call(
        _mm_kernel if res is None else _mm_res_kernel,
        grid=(m // tm, n // tn),
        in_specs=in_specs,
        out_specs=pl.BlockSpec((tm, tn), lambda i, j: (i, j)),
        out_shape=jax.ShapeDtypeStruct((m, n), F32),
        compiler_params=_cparams("parallel", "arbitrary"),
        name=name,
    )(*args)


def rope_tables(pos, rot_dim, period):
    half = rot_dim // 2
    inv_freq = 1.0 / (ROPE_THETA ** (jnp.arange(half, dtype=F32) * (2.0 / rot_dim)))
    ang = pos.astype(F32)[:, None] * inv_freq[None, :]
    cos, sin = jnp.cos(ang), jnp.sin(ang)
    t = pos.shape[0]
    rest = period - rot_dim
    c = jnp.concatenate([cos, cos, jnp.ones((t, rest), F32)], axis=1)
    s1 = jnp.concatenate([-sin, jnp.zeros((t, half + rest), F32)], axis=1)
    s2 = jnp.concatenate([jnp.zeros((t, half), F32), sin, jnp.zeros((t, rest), F32)], axis=1)
    rep = LANES // period
    return tuple(jnp.tile(x, (1, rep)) for x in (c, s1, s2))


def _rope(y, c, s1, s2, half):
    return y * c + pltpu.roll(y, LANES - half, 1) * s1 + pltpu.roll(y, half, 1) * s2


def _qkv_post_kernel(p_ref, qg_ref, kg_ref, c_ref, s1_ref, s2_ref,
                     q_ref, kf_ref, kb_ref, vf_ref, vb_ref, *, n_heads, q_scale):
    c, s1, s2 = c_ref[...], s1_ref[...], s2_ref[...]

    def normrope(x, g):
        y = x * lax.rsqrt(jnp.mean(x * x, axis=-1, keepdims=True) + RMS_EPS) * g
        return _rope(y, c, s1, s2, ROT_DIM // 2)

    qg, kg = qg_ref[...], kg_ref[...]
    for h in range(n_heads):
        sl = slice(h * HEAD_DIM, (h + 1) * HEAD_DIM)
        q_ref[:, sl] = (normrope(p_ref[:, sl], qg) * q_scale).astype(BF16)
    for h in range(N_KV):
        sl = slice(h * HEAD_DIM, (h + 1) * HEAD_DIM)
        k = normrope(p_ref[:, (n_heads + h) * HEAD_DIM:(n_heads + h + 1) * HEAD_DIM], kg)
        kf_ref[:, h, :] = k
        kb_ref[:, sl] = k.astype(BF16)
        v = p_ref[:, (n_heads + N_KV + h) * HEAD_DIM:(n_heads + N_KV + h + 1) * HEAD_DIM]
        vf_ref[:, h, :] = v
        vb_ref[:, sl] = v.astype(BF16)


def qkv_post(proj, q_gain, k_gain, tables, n_heads, q_scale):
    t, ncol = proj.shape
    tr = _pick_tile(t, 256, 16)
    kvc = N_KV * HEAD_DIM
    qc = n_heads * HEAD_DIM
    row = lambda w: pl.BlockSpec((tr, w), lambda i: (i, 0))
    row3 = pl.BlockSpec((tr, N_KV, HEAD_DIM), lambda i: (i, 0, 0))
    vec = pl.BlockSpec((1, HEAD_DIM), lambda i: (0, 0))
    cache = jax.ShapeDtypeStruct((t, N_KV, HEAD_DIM), F32)
    return pl.pallas_call(
        functools.partial(_qkv_post_kernel, n_heads=n_heads, q_scale=q_scale),
        grid=(t // tr,),
        in_specs=[row(ncol), vec, vec, row(LANES), row(LANES), row(LANES)],
        out_specs=[row(qc), row3, row(kvc), row3, row(kvc)],
        out_shape=[jax.ShapeDtypeStruct((t, qc), BF16),
                   cache, jax.ShapeDtypeStruct((t, kvc), BF16),
                   cache, jax.ShapeDtypeStruct((t, kvc), BF16)],
        compiler_params=_cparams("parallel"),
        name="qkv_post",
    )(proj, q_gain.reshape(1, HEAD_DIM).astype(F32), k_gain.reshape(1, HEAD_DIM).astype(F32), *tables)


def qkv_project(xn, w, q_gain, k_gain, tables, n_heads, q_scale):
    n_qkv = (n_heads + 2 * N_KV) * HEAD_DIM
    proj = matmul(xn, w[:, :n_qkv].astype(BF16), name="qkv_proj")
    return qkv_post(proj, q_gain, k_gain, tables, n_heads, q_scale)


SAMPLE_KEY_ROWS = 256


def _sample_keys_kernel(c_ref, n_ref, o_ref, *, n_cache_tiles, n_new):
    t = pl.program_id(1)

    @pl.when(t < n_cache_tiles)
    def _():
        for h in range(N_KV):
            o_ref[:, h * HEAD_DIM:(h + 1) * HEAD_DIM] = c_ref[:, h, :].astype(BF16)

    @pl.when(t == n_cache_tiles)
    def _():
        o_ref[:n_new, :] = n_ref[...]
        o_ref[n_new:, :] = jnp.zeros((o_ref.shape[0] - n_new, o_ref.shape[1]), BF16)

    @pl.when(t > n_cache_tiles)
    def _():
        o_ref[...] = jnp.zeros(o_ref.shape, BF16)


def sample_keys(cache, new, row0, n_new, lk):
    nb, past = cache.shape[0], cache.shape[1]
    tr = SAMPLE_KEY_ROWS
    assert past % tr == 0 and lk % tr == 0 and lk > past and n_new < tr and n_new % 16 == 0 and row0 % n_new == 0
    n_cache_tiles = past // tr
    width = N_KV * HEAD_DIM
    return pl.pallas_call(
        functools.partial(_sample_keys_kernel, n_cache_tiles=n_cache_tiles, n_new=n_new),
        grid=(nb, lk // tr),
        in_specs=[pl.BlockSpec((None, tr, N_KV, HEAD_DIM),
                               lambda b, t: (b, jnp.minimum(t, n_cache_tiles - 1), 0, 0)),
                  pl.BlockSpec((n_new, width), lambda b, t: (row0 // n_new + b, 0))],
        out_specs=pl.BlockSpec((tr, width), lambda b, t: (b * (lk // tr) + t, 0)),
        out_shape=jax.ShapeDtypeStruct((nb * lk, width), BF16),
        compiler_params=_cparams("parallel", "arbitrary"),
        name="sample_keys",
    )(cache, new)


IDX_W_SCALE = (IDX_HEADS ** -0.5) * (IDX_DIM ** -0.5)


def _idx_post_kernel(r_ref, g_ref, c_ref, s1_ref, s2_ref, qi_ref, tail_ref, kb_ref):
    c, s1, s2 = c_ref[...], s1_ref[...], s2_ref[...]
    nq = IDX_HEADS * IDX_DIM
    for j in range(nq // LANES):
        sl = slice(j * LANES, (j + 1) * LANES)
        qi_ref[:, sl] = _rope(r_ref[:, sl], c, s1, s2, IDX_ROT // 2).astype(BF16)
    t = r_ref[:, nq:nq + LANES]
    lane = lax.broadcasted_iota(I32, t.shape, 1)
    is_k = lane < IDX_DIM
    ms = jnp.sum(jnp.where(is_k, t * t, 0.0), axis=-1, keepdims=True) * (1.0 / IDX_DIM)
    kn = t * lax.rsqrt(ms + RMS_EPS) * g_ref[...]
    tail = jnp.where(is_k, _rope(kn, c, s1, s2, IDX_ROT // 2), t * IDX_W_SCALE)
    tail_ref[...] = tail
    kb_ref[...] = tail[:, :IDX_DIM].astype(BF16)


def idx_post(rest, kidx_gain, tables):
    t, ncol = rest.shape
    tr = _pick_tile(t, 256, 16)
    nq = IDX_HEADS * IDX_DIM
    row = lambda w: pl.BlockSpec((tr, w), lambda i: (i, 0))
    g = jnp.concatenate([kidx_gain.astype(F32), jnp.zeros((LANES - IDX_DIM,), F32)]).reshape(1, LANES)
    return pl.pallas_call(
        _idx_post_kernel,
        grid=(t // tr,),
        in_specs=[row(ncol), pl.BlockSpec((1, LANES), lambda i: (0, 0)), row(LANES), row(LANES), row(LANES)],
        out_specs=[row(nq), row(LANES), row(IDX_DIM)],
        out_shape=[jax.ShapeDtypeStruct((t, nq), BF16), jax.ShapeDtypeStruct((t, LANES), F32),
                   jax.ShapeDtypeStruct((t, IDX_DIM), BF16)],
        compiler_params=_cparams("parallel"),
        name="idx_post",
    )(rest, g, *tables)


def _sink_attn(q, k, v, bias, sink):
    s = lax.dot_general(q, k, _NT, preferred_element_type=F32)
    if bias is not None:
        s = s + bias
    m = jnp.maximum(jnp.max(s, axis=-1, keepdims=True), sink)
    p = jnp.exp(s - m)
    denom = jnp.sum(p, axis=-1, keepdims=True) + jnp.exp(sink - m)
    o = jnp.dot(p.astype(BF16), v, preferred_element_type=F32)
    return o / denom


def _swa_prompt_kernel(sink_ref, q_ref, kp_ref, kc_ref, vp_ref, vc_ref, o_ref, *, group):
    i = pl.program_id(1)
    rows, cols = QBLOCK, 2 * QBLOCK
    qc = lax.broadcasted_iota(I32, (rows, cols), 0) // CHUNK
    kc = lax.broadcasted_iota(I32, (rows, cols), 1) // CHUNK
    first_valid = jnp.where(i > 0, 0, QBLOCK // CHUNK)
    vis = (kc >= qc) & (kc <= qc + WINDOW // CHUNK) & (kc >= first_valid)
    bias = jnp.where(vis, 0.0, NEG)
    for kv in range(N_KV):
        sl = slice(kv * HEAD_DIM, (kv + 1) * HEAD_DIM)
        k = jnp.concatenate([kp_ref[:, sl], kc_ref[:, sl]], axis=0)
        v = jnp.concatenate([vp_ref[:, sl], vc_ref[:, sl]], axis=0)
        for g in range(group):
            h = kv * group + g
            hs = slice(h * HEAD_DIM, (h + 1) * HEAD_DIM)
            o_ref[:, hs] = _sink_attn(q_ref[:, hs], k, v, bias, sink_ref[h]).astype(BF16)


def swa_prompt(q, k, v, sinks, n_batch, seq, n_heads):
    nqb = seq // QBLOCK
    qc = n_heads * HEAD_DIM
    kvc = N_KV * HEAD_DIM
    cur = lambda w: pl.BlockSpec((QBLOCK, w), lambda b, i: (b * nqb + i, 0))
    prev = lambda w: pl.BlockSpec((QBLOCK, w), lambda b, i: (b * nqb + jnp.maximum(i - 1, 0), 0))
    return pl.pallas_call(
        functools.partial(_swa_prompt_kernel, group=n_heads // N_KV),
        grid=(n_batch, nqb),
        in_specs=[pl.BlockSpec(memory_space=pltpu.SMEM),
                  cur(qc), prev(kvc), cur(kvc), prev(kvc), cur(kvc)],
        out_specs=cur(qc),
        out_shape=jax.ShapeDtypeStruct((n_batch * seq, qc), BF16),
        compiler_params=_cparams("parallel", "arbitrary"),
        name="swa_prompt",
    )(sinks.astype(F32), q, k, k, v, v)


def _swa_sample_kernel(sink_ref, q_ref, k_ref, v_ref, o_ref, *, group):
    for kv in range(N_KV):
        sl = slice(kv * HEAD_DIM, (kv + 1) * HEAD_DIM)
        k, v = k_ref[:, sl], v_ref[:, sl]
        for g in range(group):
            h = kv * group + g
            hs = slice(h * HEAD_DIM, (h + 1) * HEAD_DIM)
            o_ref[:, hs] = _sink_attn(q_ref[:, hs], k, v, None, sink_ref[h]).astype(BF16)


def swa_sample(q, k_all, v_all, sinks, row0, n_batch, n_new, n_heads):
    qc = n_heads * HEAD_DIM
    nk = k_all.shape[1]
    kvc = N_KV * HEAD_DIM
    return pl.pallas_call(
        functools.partial(_swa_sample_kernel, group=n_heads // N_KV),
        grid=(n_batch,),
        in_specs=[pl.BlockSpec(memory_space=pltpu.SMEM),
                  pl.BlockSpec((n_new, qc), lambda b: (row0 // n_new + b, 0)),
                  pl.BlockSpec((None, nk, kvc), lambda b: (b, 0, 0)),
                  pl.BlockSpec((None, nk, kvc), lambda b: (b, 0, 0))],
        out_specs=pl.BlockSpec((n_new, qc), lambda b: (b, 0)),
        out_shape=jax.ShapeDtypeStruct((n_batch * n_new, qc), BF16),
        compiler_params=_cparams("parallel"),
        name="swa_sample",
    )(sinks.astype(F32), q, k_all, v_all)


def _select_kernel(nkt_ref, qi_ref, w_ref, lim_ref, kidx_ref, bias_ref, keys_ref, p_ref,
                   *, topk, n_tiles, idx_bits):
    tk = KEY_TILE
    blk = pl.program_id(0) * pl.num_programs(1) + pl.program_id(1)
    nkt = nkt_ref[blk]
    qi = qi_ref[...]
    w = w_ref[...]
    lim = lim_ref[...]
    imin = jnp.int32(INT_MIN)

    def tile_rows(kt):
        return pl.ds(pl.multiple_of(kt * tk, tk), tk)

    def key_index(kt):
        return kt * tk + lax.broadcasted_iota(I32, (tk, LANES), 0)

    def score_tile(kt, carry):
        d = lax.dot_general(kidx_ref[tile_rows(kt), :], qi, _NT, preferred_element_type=F32)
        r = jnp.maximum(d, 0.0) * w
        acc = r[:, 0:LANES]
        for h in range(1, IDX_HEADS):
            acc = acc + r[:, h * LANES:(h + 1) * LANES]
        bits = pltpu.bitcast(acc, I32)
        key = bits ^ ((bits >> 31) & jnp.int32(0x7FFFFFFF))
        keys_ref[tile_rows(kt), :] = jnp.where(key_index(kt) < lim, key, imin)
        return carry

    lax.fori_loop(0, nkt, score_tile, 0)

    per_trip = COUNT_ROWS // tk
    n_trips = (nkt + per_trip - 1) // per_trip

    def inadmissible_tile(kt, carry):
        keys_ref[tile_rows(kt), :] = jnp.full((tk, LANES), imin, I32)
        return carry

    lax.fori_loop(nkt, n_trips * per_trip, inadmissible_tile, 0)

    def count(pred):
        def body(j, acc):
            rows = pl.ds(pl.multiple_of(j * COUNT_ROWS, COUNT_ROWS), COUNT_ROWS)
            index = j * COUNT_ROWS + lax.broadcasted_iota(I32, (COUNT_ROWS, LANES), 0)
            c = jnp.where(pred(keys_ref[rows, :], index), 1, 0).astype(I32)
            parts = [c[r * 8:(r + 1) * 8, :] for r in range(COUNT_ROWS // 8)]
            while len(parts) > 1:
                parts = [parts[a] + parts[a + 1] for a in range(0, len(parts), 2)]
            return acc + parts[0]
        acc = lax.fori_loop(0, n_trips, body, jnp.zeros((8, LANES), I32))
        return jnp.sum(acc, axis=0, keepdims=True)

    def search_bit(i, state):
        v, settled = state
        cand = v + lax.shift_left(jnp.int32(1), 31 - i)
        cnt = count(lambda kk, _: kk >= cand)
        v = jnp.where((settled == 0) & (cnt >= topk), cand, v)
        return v, jnp.where(cnt == topk, 1, settled)

    def search_group(state):
        g, v, settled, _ = state
        v, settled = lax.fori_loop(g * SEARCH_GROUP, (g + 1) * SEARCH_GROUP, search_bit, (v, settled))
        return g + 1, v, settled, jnp.min(settled)

    settled0 = jnp.where(lim < topk, 1, 0).astype(I32)
    _, tau, _, _ = lax.while_loop(lambda s: (s[0] < 32 // SEARCH_GROUP) & (s[3] == 0), search_group,
                                  (jnp.int32(0), jnp.full((1, LANES), imin, I32), settled0, jnp.min(settled0)))

    n_ge = count(lambda kk, _: (kk >= tau) & (kk != imin))
    n_gt = count(lambda kk, _: kk > tau)
    need = topk - n_gt

    p_ref[...] = jnp.full((1, LANES), 2 ** 30, I32)

    @pl.when(jnp.max(n_ge) > topk)
    def _():
        def index_bit(i, p):
            cand = p + lax.shift_left(jnp.int32(1), idx_bits - 1 - i)
            cnt = count(lambda kk, si: (kk == tau) & (si < cand))
            return jnp.where(cnt < need, cand, p)
        p_ref[...] = lax.fori_loop(0, idx_bits, index_bit, jnp.zeros((1, LANES), I32))

    p_last = p_ref[...]

    def write_tile(kt, carry):
        kk = keys_ref[tile_rows(kt), :]
        sel = (kk > tau) | ((kk == tau) & (key_index(kt) <= p_last) & (kk != imin))
        bias_ref[tile_rows(kt), :] = jnp.where(sel, 0.0, NEG).astype(BF16)
        return carry

    lax.fori_loop(0, nkt, write_tile, 0)

    def fill_tile(kt, carry):
        bias_ref[tile_rows(kt), :] = jnp.full((tk, LANES), NEG, BF16)
        return carry

    lax.fori_loop(nkt, n_tiles, fill_tile, 0)


def dsa_select(nkt, qi_blocks, w_blocks, limits, kidx, nb, lk, topk):
    nq = qi_blocks.shape[0] // nb
    assert lk % COUNT_ROWS == 0, "the counting passes walk whole COUNT_ROWS-row trips"
    n_tiles = lk // KEY_TILE
    idx_bits = max(1, int(lk).bit_length())
    per_block = lambda shape: pl.BlockSpec((None,) + shape, lambda b, q, *_: (b * nq + q, 0, 0))
    grid_spec = pltpu.PrefetchScalarGridSpec(
        num_scalar_prefetch=1,
        grid=(nb, nq),
        in_specs=[per_block((IDX_HEADS * QBLOCK, IDX_DIM)),
                  per_block((1, IDX_HEADS * QBLOCK)),
                  per_block((1, LANES)),
                  pl.BlockSpec((lk, IDX_DIM), lambda b, q, *_: (b, 0))],
        out_specs=per_block((lk, LANES)),
        scratch_shapes=[pltpu.VMEM((lk, LANES), I32), pltpu.VMEM((1, LANES), I32)],
    )
    return pl.pallas_call(
        functools.partial(_select_kernel, topk=topk, n_tiles=n_tiles, idx_bits=idx_bits),
        grid_spec=grid_spec,
        out_shape=jax.ShapeDtypeStruct((nb * nq, lk, LANES), BF16),
        compiler_params=_cparams("parallel", "arbitrary"),
        name="dsa_select",
    )(nkt, qi_blocks, w_blocks, limits, kidx)


def _masked_attn_kernel(nkt_ref, q_ref, k_ref, v_ref, bias_ref, o_ref, m_ref, acc_ref, s0_ref, s1_ref,
                        *, group, n_tiles):
    tk = ATT_TILE
    blk = pl.program_id(0) * pl.num_programs(2) + pl.program_id(2)
    nkt = nkt_ref[blk]
    rows = group * QBLOCK
    gw = group * HEAD_DIM
    heads = range(HEADS_PER_STEP)
    eye = jnp.where(lax.broadcasted_iota(I32, (QBLOCK, QBLOCK), 0)
                    == lax.broadcasted_iota(I32, (QBLOCK, QBLOCK), 1), 1.0, 0.0).astype(BF16)
    q_aug = [jnp.concatenate(
        [jnp.concatenate([q_ref[:, hh * gw + g * HEAD_DIM:hh * gw + (g + 1) * HEAD_DIM], eye], axis=1)
         for g in range(group)], axis=0) for hh in heads]
    ones = jnp.ones((tk, HEAD_DIM), BF16)
    m_ref[...] = jnp.full(m_ref.shape, NEG, F32)
    acc_ref[...] = jnp.zeros(acc_ref.shape, F32)

    def tile_rows(kt):
        return pl.ds(pl.multiple_of(jnp.minimum(kt, n_tiles - 1) * tk, tk), tk)

    def head_cols(hh):
        return slice(hh * HEAD_DIM, (hh + 1) * HEAD_DIM)

    def scores(s_ref, kt):
        sl = tile_rows(kt)
        bias = bias_ref[sl, :]
        for hh in heads:
            k_aug = jnp.concatenate([k_ref[sl, head_cols(hh)], bias], axis=1)
            s_ref[hh] = lax.dot_general(q_aug[hh], k_aug, _NT, preferred_element_type=F32)

    def update(s_ref, kt):
        sl = tile_rows(kt)
        for hh in heads:
            s = s_ref[hh]
            m_prev = m_ref[hh]
            m_new = jnp.maximum(m_prev, jnp.max(s, axis=-1, keepdims=True))
            alpha = jnp.exp2(m_prev - m_new)
            p = jnp.exp2(s - jnp.concatenate([m_new] * (tk // LANES), axis=1))
            v_aug = jnp.concatenate([v_ref[sl, head_cols(hh)], ones], axis=1)
            pv = jnp.dot(p.astype(BF16), v_aug, preferred_element_type=F32)
            acc_ref[hh] = jnp.concatenate([alpha, alpha], axis=1) * acc_ref[hh] + pv
            m_ref[hh] = m_new

    scores(s0_ref, 0)

    def run(first, n_trips, per_trip):
        def body(u, carry):
            base = first + per_trip * u
            for i in range(0, per_trip, 2):
                scores(s1_ref, base + i + 1)
                update(s0_ref, base + i)
                scores(s0_ref, base + i + 2)
                update(s1_ref, base + i + 1)
            return carry
        lax.fori_loop(0, n_trips, body, 0)

    n_long = nkt // LONG_TRIP
    run(0, n_long, LONG_TRIP)
    run(n_long * LONG_TRIP, (nkt - n_long * LONG_TRIP + 1) // 2, 2)
    for hh in heads:
        acc = acc_ref[hh]
        o = acc[:, :HEAD_DIM] / jnp.maximum(acc[:, HEAD_DIM:], 1e-30)
        for g in range(group):
            o_ref[:, hh * gw + g * HEAD_DIM:hh * gw + (g + 1) * HEAD_DIM] = (
                o[g * QBLOCK:(g + 1) * QBLOCK].astype(BF16))


def dsa_attend(nkt, q, k, v, bias, nb, n_heads):
    lk = bias.shape[1]
    nq = bias.shape[0] // nb
    group = n_heads // N_KV
    gw = group * HEAD_DIM
    rows = group * QBLOCK
    n_tiles = lk // ATT_TILE
    assert lk % (2 * ATT_TILE) == 0, "the tile-pair loop needs an even number of key tiles"
    hps = HEADS_PER_STEP
    qspec = pl.BlockSpec((QBLOCK, hps * gw), lambda b, h, q, *_: (b * nq + q, h))
    kvspec = pl.BlockSpec((lk, hps * HEAD_DIM), lambda b, h, q, *_: (b, h))
    grid_spec = pltpu.PrefetchScalarGridSpec(
        num_scalar_prefetch=1,
        grid=(nb, N_KV // hps, nq),
        in_specs=[qspec, kvspec, kvspec,
                  pl.BlockSpec((None, lk, LANES), lambda b, h, q, *_: (b * nq + q, 0, 0))],
        out_specs=qspec,
        scratch_shapes=[pltpu.VMEM((hps, rows, LANES), F32),
                        pltpu.VMEM((hps, rows, 2 * HEAD_DIM), F32),
                        pltpu.VMEM((hps, rows, ATT_TILE), F32),
                        pltpu.VMEM((hps, rows, ATT_TILE), F32)],
    )
    return pl.pallas_call(
        functools.partial(_masked_attn_kernel, group=group, n_tiles=n_tiles),
        grid_spec=grid_spec,
        out_shape=jax.ShapeDtypeStruct((nb * nq * QBLOCK, q.shape[1]), BF16),
        compiler_params=_cparams("parallel", "parallel", "arbitrary"),
        name="dsa_attend",
    )(nkt, q, k, v, bias)


def _router_kernel(l_ref, b_ref, id_ref, gate_ref):
    x = l_ref[...] + b_ref[...]
    lane = lax.broadcasted_iota(I32, x.shape, 1)
    big = jnp.int32(2 ** 30)
    ninf = -jnp.inf

    def first_max(vals):
        m = jnp.max(vals, axis=-1, keepdims=True)
        return m, jnp.min(jnp.where(vals == m, lane, big), axis=-1, keepdims=True)

    glog = jnp.where(lane < N_GROUPS, x, ninf)
    gmax, gsel = first_max(glog)
    gprob = 1.0 / jnp.sum(jnp.exp(glog - gmax), axis=-1, keepdims=True)
    in_group = (lane >= N_GROUPS) & ((lane - N_GROUPS) // EXP_PER_GROUP == gsel) & (lane < N_GROUPS + N_EXPERTS)
    e1 = jnp.where(in_group, x, ninf)
    v1, i1 = first_max(e1)
    v2, i2 = first_max(jnp.where(lane == i1, ninf, e1))
    t = jnp.exp(v2 - v1)
    g1 = gprob / (1.0 + t)
    g2 = gprob * t / (1.0 + t)
    id_ref[...] = jnp.where(lane == 0, i1 - N_GROUPS, jnp.where(lane == 1, i2 - N_GROUPS, 0))
    gate_ref[...] = jnp.where(lane == 0, g1, jnp.where(lane == 1, g2, 0.0))


def router(logits, bias):
    t = logits.shape[0]
    tr = _pick_tile(t, 256, 8)
    row = pl.BlockSpec((tr, LANES), lambda i: (i, 0))
    return pl.pallas_call(
        _router_kernel,
        grid=(t // tr,),
        in_specs=[row, pl.BlockSpec((1, LANES), lambda i: (0, 0))],
        out_specs=[row, row],
        out_shape=[jax.ShapeDtypeStruct((t, LANES), I32), jax.ShapeDtypeStruct((t, LANES), F32)],
        compiler_params=_cparams("parallel"),
        name="router",
    )(logits, bias)


DMA_UNROLL = 8


def _row_copy(src_hbm, row, dst_ref, r, sem):
    return pltpu.make_async_copy(src_hbm.at[pl.ds(row, 1), :], dst_ref.at[pl.ds(r, 1), :], sem)


def _start_row_gather(src_hbm, row_of, dst_ref, sem, n_rows, beside_block_loads=False):
    def trip(j, carry):
        for u in range(DMA_UNROLL):
            r = j * DMA_UNROLL + u
            _row_copy(src_hbm, row_of(r), dst_ref, r, sem).start(priority=1 if beside_block_loads else u % 2)
        return carry
    lax.fori_loop(0, n_rows // DMA_UNROLL, trip, 0)


def _wait_row_gather(src_hbm, dst_ref, sem, n_rows):
    pltpu.make_async_copy(src_hbm.at[pl.ds(0, n_rows), :], dst_ref, sem).wait()


def _experts_kernel(tok_ref, bexp_ref, nused_ref, x_hbm, wg_ref, wu_ref, wd_ref, o_ref, xbuf_ref, sem_ref):
    i, j = pl.program_id(0), pl.program_id(1)
    n_used = nused_ref[0]
    slot = i % 2
    first = j == 0

    def start_block(block, buf):
        base = block * MOE_ROWS
        _start_row_gather(x_hbm, lambda r: tok_ref[base + r], xbuf_ref.at[buf], sem_ref.at[buf], MOE_ROWS,
                          beside_block_loads=True)

    @pl.when(first & (i == 0))
    def _():
        start_block(0, 0)

    @pl.when(first & (i + 1 < n_used))
    def _():
        start_block(i + 1, 1 - slot)

    @pl.when(first & (i < n_used))
    def _():
        _wait_row_gather(x_hbm, xbuf_ref.at[slot], sem_ref.at[slot], MOE_ROWS)

    @pl.when(i < n_used)
    def _():
        x0, x1 = _unpack_rows(xbuf_ref[slot])
        half = x0.shape[1]

        def proj(w_ref):
            return (jnp.dot(x0, w_ref[:half, :].astype(BF16), preferred_element_type=F32)
                    + jnp.dot(x1, w_ref[half:, :].astype(BF16), preferred_element_type=F32))

        a = proj(wg_ref)
        b = proj(wu_ref)
        hid = (a * (1.0 / (1.0 + jnp.exp(-a)))) * b
        y = jnp.dot(hid.astype(BF16), wd_ref[...].astype(BF16), preferred_element_type=F32)

        @pl.when(first)
        def _():
            o_ref[...] = y

        @pl.when(jnp.logical_not(first))
        def _():
            o_ref[...] += y

    @pl.when(first & (i >= n_used))
    def _():
        o_ref[...] = jnp.zeros(o_ref.shape, o_ref.dtype)


def grouped_experts(tok_buf, blk_exp, n_used, x, w_gate, w_up, w_down, layer, n_blocks):
    d, f = w_gate.shape[2], w_gate.shape[3]
    fh = f // 2

    def expert_of(i, be, nu):
        return be[jnp.minimum(i, nu[0] - 1)]

    def half_of(i, j, nu):
        ii = jnp.minimum(i, nu[0] - 1)
        jj = jnp.where(i < nu[0], j, 1)
        return jnp.where(ii % 2 == 0, jj, 1 - jj)

    w_in = pl.BlockSpec((None, None, d, fh),
                        lambda i, j, tok, be, nu: (layer, expert_of(i, be, nu), 0, half_of(i, j, nu)))
    w_out = pl.BlockSpec((None, None, fh, d),
                         lambda i, j, tok, be, nu: (layer, expert_of(i, be, nu), half_of(i, j, nu), 0))
    grid_spec = pltpu.PrefetchScalarGridSpec(
        num_scalar_prefetch=3,
        grid=(n_blocks, 2),
        in_specs=[pl.BlockSpec(memory_space=pl.ANY), w_in, w_in, w_out],
        out_specs=pl.BlockSpec((MOE_ROWS, d), lambda i, j, *_: (i, 0)),
        scratch_shapes=[pltpu.VMEM((2, MOE_ROWS, x.shape[1]), x.dtype), pltpu.SemaphoreType.DMA((2,))],
    )
    return pl.pallas_call(
        _experts_kernel,
        grid_spec=grid_spec,
        out_shape=jax.ShapeDtypeStruct((n_blocks * MOE_ROWS, d), F32),
        compiler_params=_cparams("arbitrary", "arbitrary"),
        name="moe_experts",
    )(tok_buf, blk_exp, n_used, x, w_gate, w_up, w_down)


def _combine_kernel(p0_ref, p1_ref, h_ref, g_ref, y_hbm, *rest, rows, split_step):
    o_refs, (buf_ref, sem_ref) = rest[:-2], rest[-2:]
    i = pl.program_id(0)
    base = i * rows
    _start_row_gather(y_hbm, lambda r: p0_ref[base + r], buf_ref.at[0], sem_ref.at[0], rows)
    _start_row_gather(y_hbm, lambda r: p1_ref[base + r], buf_ref.at[1], sem_ref.at[1], rows)
    g = g_ref[...]
    g0, g1 = g[:, 0:1], g[:, 1:2]
    _wait_row_gather(y_hbm, buf_ref.at[0], sem_ref.at[0], rows)
    _wait_row_gather(y_hbm, buf_ref.at[1], sem_ref.at[1], rows)
    out = h_ref[...] + (g0 * buf_ref[0] + g1 * buf_ref[1])
    if split_step is None:
        o_refs[0][...] = out
    else:
        @pl.when(i < split_step)
        def _():
            o_refs[0][...] = out

        @pl.when(i >= split_step)
        def _():
            o_refs[1][...] = out


def combine(pos0, pos1, h, gates, y_sorted, split_rows=None):
    t, d = h.shape
    rows = _pick_tile(t if split_rows is None else math.gcd(split_rows, t - split_rows), 128, 8)
    row = lambda w: pl.BlockSpec((rows, w), lambda i, *_: (i, 0))
    if split_rows is None:
        split_step = None
        out_specs = row(d)
        out_shape = jax.ShapeDtypeStruct((t, d), F32)
    else:
        split_step = split_rows // rows
        out_specs = [pl.BlockSpec((rows, d), lambda i, *_: (jnp.minimum(i, split_step - 1), 0)),
                     pl.BlockSpec((rows, d), lambda i, *_: (jnp.maximum(i - split_step, 0), 0))]
        out_shape = [jax.ShapeDtypeStruct((split_rows, d), F32), jax.ShapeDtypeStruct((t - split_rows, d), F32)]
    grid_spec = pltpu.PrefetchScalarGridSpec(
        num_scalar_prefetch=2,
        grid=(t // rows,),
        in_specs=[row(d), row(LANES), pl.BlockSpec(memory_space=pl.ANY)],
        out_specs=out_specs,
        scratch_shapes=[pltpu.VMEM((2, rows, d), F32), pltpu.SemaphoreType.DMA((2,))],
    )
    return pl.pallas_call(
        functools.partial(_combine_kernel, rows=rows, split_step=split_step),
        grid_spec=grid_spec,
        out_shape=out_shape,
        compiler_params=_cparams("arbitrary"),
        name="moe_combine",
    )(pos0, pos1, h, gates, y_sorted)


def _dispatch_plan(expert, n_tokens):
    a = expert.shape[0]
    order = jnp.argsort(expert, stable=True).astype(I32)
    rank = jnp.argsort(order).astype(I32)
    experts = jnp.arange(N_EXPERTS, dtype=I32)
    counts = jnp.sum((expert[:, None] == experts[None, :]).astype(I32), axis=0)
    start = jnp.cumsum(counts) - counts
    pcounts = (counts + MOE_ROWS - 1) // MOE_ROWS * MOE_ROWS
    pend = jnp.cumsum(pcounts)
    pstart = pend - pcounts
    slot = pstart[expert] + (rank - start[expert])
    n_blocks = (a + MOE_ROWS - 1) // MOE_ROWS + N_EXPERTS
    block_row0 = jnp.arange(n_blocks, dtype=I32) * MOE_ROWS
    blk_exp = jnp.minimum(jnp.sum((pend[None, :] <= block_row0[:, None]).astype(I32), axis=1),
                          N_EXPERTS - 1).astype(I32)
    slot_exp = jnp.repeat(blk_exp, MOE_ROWS)
    within = jnp.arange(n_blocks * MOE_ROWS, dtype=I32) - pstart[slot_exp]
    src = order[jnp.clip(start[slot_exp] + within, 0, a - 1)]
    tok_buf = jnp.where((within >= 0) & (within < counts[slot_exp]), src // 2, 0).astype(I32)
    n_used = (pend[-1] // MOE_ROWS).astype(I32).reshape(1)
    return tok_buf, slot.astype(I32), blk_exp, n_used, n_blocks


def moe_layer(h, norm_g, w_group, b_group, w_expert, b_expert, w_gate, w_up, w_down, layer, split_rows=None):
    t, d = h.shape
    xn_b, xn_p = rmsnorm_pack(h, norm_g)
    n_logit = N_GROUPS + N_EXPERTS
    w_r = jnp.concatenate([w_group, w_expert, jnp.zeros((d, LANES - n_logit), F32)], axis=1).astype(BF16)
    b_r = jnp.concatenate([b_group.astype(F32), b_expert.astype(F32),
                           jnp.zeros((LANES - n_logit,), F32)]).reshape(1, LANES)
    ids, gates = router(matmul(xn_b, w_r, name="router_logits"), b_r)
    tok_buf, slot, blk_exp, n_used, n_blocks = _dispatch_plan(ids[:, :2].reshape(-1), t)
    y_sorted = grouped_experts(tok_buf, blk_exp, n_used, xn_p, w_gate, w_up, w_down, layer, n_blocks)
    slot2 = slot.reshape(t, 2)
    return combine(slot2[:, 0], slot2[:, 1], h, gates, y_sorted, split_rows)


def _query_blocks(x, n_blocks, width):
    return x.reshape(n_blocks, QBLOCK, IDX_HEADS, width).transpose(0, 2, 1, 3).reshape(
        n_blocks, IDX_HEADS * QBLOCK, width)


def kernel(x_prompt, x_sample, cache_a_k, cache_a_v, cache_b_k, cache_b_v, cache_b_kidx, attn_norm, ffn_norm, a_w_qkv, a_w_o, a_q_norm, a_k_norm, a_sinks, b_w_qkv, b_w_o, b_q_norm, b_k_norm, b_kidx_norm, moe_w_group, moe_b_group, moe_w_expert, moe_b_expert, moe_w_gate, moe_w_up, moe_w_down):
    nb, seq, d = x_prompt.shape
    db, ds, _ = x_sample.shape
    depth = attn_norm.shape[0]
    n_heads = d // HEAD_DIM
    qc = n_heads * HEAD_DIM
    kvc = N_KV * HEAD_DIM
    tp, ts = nb * seq, db * ds
    past = cache_b_k.shape[2]
    nqb = seq // QBLOCK

    h = jnp.concatenate([x_prompt.reshape(tp, d), x_sample.reshape(ts, d)], axis=0)
    pos = jnp.concatenate([jnp.tile(jnp.arange(seq), nb), jnp.tile(past + jnp.arange(ds), db)])
    head_tables = rope_tables(pos, ROT_DIM, HEAD_DIM)
    idx_tables = rope_tables(pos, IDX_ROT, IDX_DIM)

    outs_a = {"kp": [], "vp": [], "ks": [], "vs": []}
    outs_b = {"kp": [], "vp": [], "ip": [], "ks": [], "vs": [], "is": []}
    for i in range(depth):
        j = i // 2
        xn = rmsnorm(h, attn_norm[i], (BF16,))[0]
        if i % 2 == 0:
            q_b, k_f, k_b, v_f, v_b = qkv_project(xn, a_w_qkv[j], a_q_norm[j], a_k_norm[j], head_tables, n_heads,
                                                  HEAD_DIM ** -0.5)
            o_p = swa_prompt(q_b, k_b, v_b, a_sinks[j], nb, seq, n_heads)
            k_all = jnp.concatenate([cache_a_k[j].astype(F32), k_f[tp:].reshape(db, ds, N_KV, HEAD_DIM)], axis=1)
            v_all = jnp.concatenate([cache_a_v[j].astype(F32), v_f[tp:].reshape(db, ds, N_KV, HEAD_DIM)], axis=1)
            o_s = swa_sample(q_b, k_all.reshape(db, WINDOW + ds, kvc).astype(BF16),
                             v_all.reshape(db, WINDOW + ds, kvc).astype(BF16), a_sinks[j], tp, db, ds, n_heads)
            outs_a["kp"].append(k_f[:tp].reshape(nb, seq, N_KV, HEAD_DIM)[:, -WINDOW:])
            outs_a["vp"].append(v_f[:tp].reshape(nb, seq, N_KV, HEAD_DIM)[:, -WINDOW:])
            outs_a["ks"].append(k_all[:, -WINDOW:])
            outs_a["vs"].append(v_all[:, -WINDOW:])
            w_o = a_w_o[j]
        else:
            n_qkv = qc + 2 * kvc
            n_rest = b_w_qkv.shape[2] - n_qkv
            rest_w = IDX_HEADS * IDX_DIM + LANES
            w_rest = jnp.pad(b_w_qkv[j][:, n_qkv:], ((0, 0), (0, rest_w - n_rest))).astype(BF16)
            rest = matmul(xn, w_rest, tn_target=384, name="idx_proj")
            q_b, k_f, k_b, v_f, v_b = qkv_project(xn, b_w_qkv[j], b_q_norm[j], b_k_norm[j], head_tables, n_heads,
                                                  HEAD_DIM ** -0.5 * LOG2E)
            qi_b, tail, kidx_b = idx_post(rest, b_kidx_norm[j], idx_tables)
            kidx_f = tail[:, :IDX_DIM]
            w_idx = tail[:, IDX_DIM:IDX_DIM + IDX_HEADS]

            blk_q = jnp.arange(nqb, dtype=I32)
            nkt_p = jnp.tile(((blk_q + 1) * QBLOCK + KEY_TILE - 1) // KEY_TILE, nb).astype(I32)
            nat_p = jnp.tile(((blk_q + 1) * QBLOCK + ATT_TILE - 1) // ATT_TILE, nb).astype(I32)
            qrow = blk_q[:, None] * QBLOCK + jnp.arange(QBLOCK, dtype=I32)[None, :]
            lim_p = jnp.tile(((qrow // CHUNK + 1) * CHUNK)[:, None, :], (nb, 1, 1)).astype(I32)
            qi_p = _query_blocks(qi_b[:tp], nb * nqb, IDX_DIM)
            w_p = _query_blocks(w_idx[:tp], nb * nqb, 1).reshape(nb * nqb, 1, IDX_HEADS * QBLOCK)
            bias_p = dsa_select(nkt_p, qi_p, w_p, lim_p, kidx_b, nb, seq, min(TOPK_MAX, seq // 4))
            o_p = dsa_attend(nat_p, q_b, k_b, v_b, bias_p, nb, n_heads)

            n_keys = past + ds
            lk = (n_keys + 2 * ATT_TILE - 1) // (2 * ATT_TILE) * (2 * ATT_TILE)
            pad_k = lambda new, cached: jnp.pad(
                jnp.concatenate([cached.reshape(db, past, -1), new.reshape(db, ds, -1)], axis=1),
                ((0, 0), (0, lk - n_keys), (0, 0))).reshape(db * lk, -1)
            pad_q = lambda x: jnp.pad(x.reshape(db, ds, -1), ((0, 0), (0, QBLOCK - ds), (0, 0))).reshape(
                db * QBLOCK, -1)
            nkt_s = jnp.full((db,), (n_keys + KEY_TILE - 1) // KEY_TILE, I32)
            nat_s = jnp.full((db,), (n_keys + ATT_TILE - 1) // ATT_TILE, I32)
            lim_s = jnp.tile(jnp.where(jnp.arange(QBLOCK) < ds, n_keys, 0).astype(I32)[None, None, :], (db, 1, 1))
            qi_s = _query_blocks(pad_q(qi_b[tp:]), db, IDX_DIM)
            w_s = _query_blocks(pad_q(w_idx[tp:]), db, 1).reshape(db, 1, IDX_HEADS * QBLOCK)
            bias_s = dsa_select(nkt_s, qi_s, w_s, lim_s, pad_k(kidx_b[tp:], cache_b_kidx[j].astype(BF16)), db, lk,
                                min(TOPK_MAX, n_keys // 4))
            o_s = dsa_attend(nat_s, pad_q(q_b[tp:]), sample_keys(cache_b_k[j], k_b, tp, ds, lk),
                             sample_keys(cache_b_v[j], v_b, tp, ds, lk), bias_s, db, n_heads)
            o_s = o_s.reshape(db, QBLOCK, qc)[:, :ds].reshape(ts, qc)

            outs_b["kp"].append(k_f[:tp].reshape(nb, seq, N_KV, HEAD_DIM))
            outs_b["vp"].append(v_f[:tp].reshape(nb, seq, N_KV, HEAD_DIM))
            outs_b["ip"].append(kidx_f[:tp].reshape(nb, seq, IDX_DIM))
            outs_b["ks"].append(k_f[tp:].reshape(db, ds, N_KV, HEAD_DIM))
            outs_b["vs"].append(v_f[tp:].reshape(db, ds, N_KV, HEAD_DIM))
            outs_b["is"].append(kidx_f[tp:].reshape(db, ds, IDX_DIM))
            w_o = b_w_o[j]
        h = matmul(jnp.concatenate([o_p, o_s], axis=0), w_o.astype(BF16), res=h, name="out_proj")
        h = moe_layer(h, ffn_norm[i], moe_w_group[i], moe_b_group[i], moe_w_expert[i], moe_b_expert[i],
                      moe_w_gate, moe_w_up, moe_w_down, i, split_rows=tp if i == depth - 1 else None)

    return (h[0].reshape(nb, seq, d), h[1].reshape(db, ds, d),
            jnp.stack(outs_a["kp"]), jnp.stack(outs_a["vp"]),
            jnp.stack(outs_b["kp"]), jnp.stack(outs_b["vp"]), jnp.stack(outs_b["ip"]),
            jnp.stack(outs_a["ks"]), jnp.stack(outs_a["vs"]),
            jnp.stack(outs_b["ks"]), jnp.stack(outs_b["vs"]), jnp.stack(outs_b["is"]))
```
